```python
import jax, jax.numpy as jnp
from jax import lax
import numpy as np

D_MODEL = 1024
BATCH = 2
SEQ = 16384
DEPTH = 4

GRID_W = 64
CTX_LEN = 256
N_HEADS = 8
QK_NOPE_DIM = 64
QK_ROPE_DIM = 32
QK_HEAD_DIM = QK_NOPE_DIM + QK_ROPE_DIM
V_HEAD_DIM = 64
Q_LORA_RANK = 256
KV_LORA_RANK = 128
MLA_WIDTH = N_HEADS * V_HEAD_DIM
ROPE_AXIS_DIM = QK_ROPE_DIM // 2
ROPE_BASE = 10000.0
Q_BLOCK = 128
FOURIER_GROUPS = 4
FOURIER_GROUP_DIM = 128
FOURIER_WIDTH = FOURIER_GROUPS * FOURIER_GROUP_DIM
MIX_WIDTH = MLA_WIDTH + FOURIER_WIDTH
IN_PROJ_WIDTH = Q_LORA_RANK + KV_LORA_RANK + QK_ROPE_DIM + FOURIER_WIDTH
N_EXPERTS = 16
N_EXPERT_GROUPS = 4
EXPERTS_PER_GROUP = N_EXPERTS // N_EXPERT_GROUPS
TOP_K = 2
D_EXPERT = 256
EXPERT_BLOCK = 128
EPS = 1e-6

kernel_name = "hymba_mla_fnet_grouped_moe_dit"


def rms_norm(x, g):
    xf = x.astype(jnp.float32)
    y = xf * lax.rsqrt(jnp.mean(xf * xf, axis=-1, keepdims=True) + EPS)
    return (y * g.astype(jnp.float32)).astype(x.dtype)


def modulate(h, shift, scale):
    return h * (1 + scale) + shift


def grid_angles(n_lat):
    rows = n_lat // GRID_W
    row = jnp.repeat(jnp.arange(rows, dtype=jnp.float32), GRID_W)
    col = jnp.tile(jnp.arange(GRID_W, dtype=jnp.float32), rows)
    half = ROPE_AXIS_DIM // 2
    freqs = ROPE_BASE ** (-jnp.arange(half, dtype=jnp.float32) / half)
    return row[:, None] * freqs, col[:, None] * freqs


def rope_1d(x, ang):
    f = x.shape[-1] // 2
    cos = jnp.cos(ang)[None, :, None, :].astype(x.dtype)
    sin = jnp.sin(ang)[None, :, None, :].astype(x.dtype)
    x1, x2 = x[..., :f], x[..., f:]
    return jnp.concatenate([x1 * cos - x2 * sin, x2 * cos + x1 * sin], axis=-1)


def rope_tail_2d(t, ang_row, ang_col):
    nope, rp = t[..., :QK_NOPE_DIM], t[..., QK_NOPE_DIM:]
    rp = jnp.concatenate([rope_1d(rp[..., :ROPE_AXIS_DIM], ang_row),
                          rope_1d(rp[..., ROPE_AXIS_DIM:], ang_col)], axis=-1)
    return jnp.concatenate([nope, rp], axis=-1)


def mla_queries(cq, q_a_norm, w_q_b, q_norm, ang_row, ang_col):
    B, N, _ = cq.shape
    q = (rms_norm(cq, q_a_norm) @ w_q_b).reshape(B, N, N_HEADS, QK_HEAD_DIM)
    q = rms_norm(q, q_norm)
    if ang_row is not None:
        q = rope_tail_2d(q, ang_row, ang_col)
    return q


def mla_keys_values(ckv, k_rope, kv_a_norm, w_kv_b, k_norm, ang_row, ang_col):
    B, N, _ = ckv.shape
    kv = (rms_norm(ckv, kv_a_norm) @ w_kv_b).reshape(B, N, N_HEADS, QK_NOPE_DIM + V_HEAD_DIM)
    k_nope, v = kv[..., :QK_NOPE_DIM], kv[..., QK_NOPE_DIM:]
    k_r = jnp.broadcast_to(k_rope[:, :, None, :], (B, N, N_HEADS, QK_ROPE_DIM))
    k = rms_norm(jnp.concatenate([k_nope, k_r], axis=-1), k_norm)
    if ang_row is not None:
        k = rope_tail_2d(k, ang_row, ang_col)
    return k, v


def blocked_attention(q, k, v):
    B, Nq, H, Dq = q.shape
    nb = Nq // Q_BLOCK
    qb = q.reshape(B, nb, Q_BLOCK, H, Dq).transpose(1, 0, 2, 3, 4)
    kt = k.transpose(0, 2, 1, 3)
    vt = v.transpose(0, 2, 1, 3)
    scale = QK_HEAD_DIM ** -0.5

    def one_block(qblk):
        s = jnp.einsum('bqhd,bhkd->bhqk', qblk, kt).astype(jnp.float32) * scale
        p = jax.nn.softmax(s, axis=-1).astype(vt.dtype)
        return jnp.einsum('bhqk,bhkd->bqhd', p, vt)

    o = lax.map(one_block, qb)
    return o.transpose(1, 0, 2, 3, 4).reshape(B, Nq, H * V_HEAD_DIM)


def fourier_mix(f, w_fourier):
    B, N, _ = f.shape
    z = f.reshape(B, N, FOURIER_GROUPS, FOURIER_GROUP_DIM).astype(jnp.float32)
    zr = jnp.fft.fft2(z, axes=(1, 3), norm="ortho").real.astype(f.dtype)
    return jnp.einsum('bngc,gcd->bngd', zr, w_fourier).reshape(B, N, FOURIER_WIDTH)


def hybrid_mixer(h_ctx, h_lat, ang_row, ang_col, w_in, q_a_norm, w_q_b, kv_a_norm, w_kv_b,
                 q_norm, k_norm, w_fourier, w_out, with_ctx_out):
    cuts = [Q_LORA_RANK, Q_LORA_RANK + KV_LORA_RANK, Q_LORA_RANK + KV_LORA_RANK + QK_ROPE_DIM]
    cq_c, ckv_c, kr_c, f_c = jnp.split(h_ctx @ w_in, cuts, axis=-1)
    cq_l, ckv_l, kr_l, f_l = jnp.split(h_lat @ w_in, cuts, axis=-1)
    k_c, v_c = mla_keys_values(ckv_c, kr_c, kv_a_norm, w_kv_b, k_norm, None, None)
    k_l, v_l = mla_keys_values(ckv_l, kr_l, kv_a_norm, w_kv_b, k_norm, ang_row, ang_col)
    k_all = jnp.concatenate([k_c, k_l], axis=1)
    v_all = jnp.concatenate([v_c, v_l], axis=1)
    q_l = mla_queries(cq_l, q_a_norm, w_q_b, q_norm, ang_row, ang_col)
    o_lat = blocked_attention(q_l, k_all, v_all)
    out_lat = jnp.concatenate([o_lat, fourier_mix(f_l, w_fourier)], axis=-1) @ w_out
    out_ctx = None
    if with_ctx_out:
        q_c = mla_queries(cq_c, q_a_norm, w_q_b, q_norm, None, None)
        o_ctx = blocked_attention(q_c, k_c, v_c)
        out_ctx = jnp.concatenate([o_ctx, fourier_mix(f_c, w_fourier)], axis=-1) @ w_out
    return out_ctx, out_lat


def grouped_moe(h, router_w, router_bias, w_gate, w_up, w_down):
    T, D = h.shape
    scores = jax.nn.sigmoid((h @ router_w).astype(jnp.float32))
    biased = (scores + router_bias.astype(jnp.float32)).reshape(T, N_EXPERT_GROUPS, EXPERTS_PER_GROUP)
    group_score = lax.top_k(biased, TOP_K)[0].sum(-1)
    grp = jnp.argmax(group_score, axis=-1).astype(jnp.int32)
    in_grp = jnp.take_along_axis(biased, grp[:, None, None], axis=1)[:, 0]
    _, local = lax.top_k(in_grp, TOP_K)
    expert = grp[:, None] * EXPERTS_PER_GROUP + local.astype(jnp.int32)
    wts = jnp.take_along_axis(scores, expert, axis=1)
    wts = wts / jnp.sum(wts, axis=-1, keepdims=True)
    A = T * TOP_K
    e = expert.reshape(A)
    tok = jnp.repeat(jnp.arange(T, dtype=jnp.int32), TOP_K)
    w = wts.reshape(A)
    order = jnp.argsort(e)
    e_s, tok_s, w_s = e[order], tok[order], w[order]
    counts = jnp.zeros((N_EXPERTS,), jnp.int32).at[e].add(1)
    padded = (counts + EXPERT_BLOCK - 1) // EXPERT_BLOCK * EXPERT_BLOCK
    pad_end = jnp.cumsum(padded)
    pad_start = pad_end - padded
    raw_start = jnp.cumsum(counts) - counts
    dest = pad_start[e_s] + jnp.arange(A, dtype=jnp.int32) - raw_start[e_s]
    n_blocks = -(-A // EXPERT_BLOCK) + N_EXPERTS
    P = n_blocks * EXPERT_BLOCK
    buf_tok = jnp.zeros((P,), jnp.int32).at[dest].set(tok_s)
    xb = h[buf_tok].reshape(n_blocks, EXPERT_BLOCK, D)
    blk_start = jnp.arange(n_blocks, dtype=jnp.int32) * EXPERT_BLOCK
    blk_e = jnp.minimum(jnp.searchsorted(pad_end, blk_start, side='right'), N_EXPERTS - 1)

    def expert_block(args):
        xblk, ei = args
        return (jax.nn.silu(xblk @ w_gate[ei]) * (xblk @ w_up[ei])) @ w_down[ei]

    yb = lax.map(expert_block, (xb, blk_e)).reshape(P, D)
    y = yb[dest] * w_s[:, None].astype(h.dtype)
    return jnp.zeros_like(h).at[tok_s].add(y)


def setup_inputs(seed: int = 0) -> dict:
    key = jax.random.key(seed)
    ks = jax.random.split(key, 24)
    D, L, E = D_MODEL, DEPTH, N_EXPERTS
    nrm = lambda k, shape, fan_in: jax.random.normal(k, shape, jnp.float32) * fan_in ** -0.5
    gain = lambda k, shape: 1.0 + 0.02 * jax.random.normal(k, shape, jnp.float32)
    return {
        "x": jax.random.normal(ks[0], (BATCH, SEQ, D), jnp.float32),
        "c": jax.random.normal(ks[1], (BATCH, D), jnp.float32),
        "ctx": jax.random.normal(ks[2], (BATCH, CTX_LEN, D), jnp.float32),
        "c_ctx": jax.random.normal(ks[3], (D,), jnp.float32),
        "w_ada": nrm(ks[4], (L, D, 6 * D), D) * 0.5,
        "b_ada": 0.01 * jax.random.normal(ks[5], (L, 6 * D), jnp.float32),
        "norm_mix": gain(ks[6], (L, D)),
        "norm_ffn": gain(ks[7], (L, D)),
        "w_in": nrm(ks[8], (L, D, IN_PROJ_WIDTH), D),
        "q_a_norm": gain(ks[9], (L, Q_LORA_RANK)),
        "w_q_b": nrm(ks[10], (L, Q_LORA_RANK, N_HEADS * QK_HEAD_DIM), Q_LORA_RANK),
        "kv_a_norm": gain(ks[11], (L, KV_LORA_RANK)),
        "w_kv_b": nrm(ks[12], (L, KV_LORA_RANK, N_HEADS * (QK_NOPE_DIM + V_HEAD_DIM)), KV_LORA_RANK),
        "q_norm": gain(ks[13], (L, QK_HEAD_DIM)),
        "k_norm": gain(ks[14], (L, QK_HEAD_DIM)),
        "w_fourier": nrm(ks[15], (L, FOURIER_GROUPS, FOURIER_GROUP_DIM, FOURIER_GROUP_DIM), FOURIER_GROUP_DIM),
        "w_out": nrm(ks[16], (L, MIX_WIDTH, D), MIX_WIDTH),
        "router_w": nrm(ks[17], (D, E), D),
        "router_bias": 0.01 * jax.random.normal(ks[18], (E,), jnp.float32),
        "w_gate": nrm(ks[19], (L, E, D, D_EXPERT), D),
        "w_up": nrm(ks[20], (L, E, D, D_EXPERT), D),
        "w_down": nrm(ks[21], (L, E, D_EXPERT, D), D_EXPERT),
    }


def reference(x, c, ctx, c_ctx, w_ada, b_ada, norm_mix, norm_ffn, w_in, q_a_norm, w_q_b,
              kv_a_norm, w_kv_b, q_norm, k_norm, w_fourier, w_out, router_w, router_bias,
              w_gate, w_up, w_down):
    B, S, D = x.shape
    n_ctx = ctx.shape[1]
    ang_row, ang_col = grid_angles(S)
    for l in range(DEPTH):
        last = l == DEPTH - 1
        mod_lat = (jax.nn.silu(c) @ w_ada[l] + b_ada[l])[:, None, :]
        mod_ctx = jax.nn.silu(c_ctx) @ w_ada[l] + b_ada[l]
        sh1, sc1, g1, sh2, sc2, g2 = jnp.split(mod_lat, 6, axis=-1)
        csh1, csc1, cg1, csh2, csc2, cg2 = jnp.split(mod_ctx, 6, axis=-1)
        h_lat = modulate(rms_norm(x, norm_mix[l]), sh1, sc1)
        h_ctx = modulate(rms_norm(ctx, norm_mix[l]), csh1, csc1)
        mix_ctx, mix_lat = hybrid_mixer(h_ctx, h_lat, ang_row, ang_col, w_in[l], q_a_norm[l], w_q_b[l],
                                        kv_a_norm[l], w_kv_b[l], q_norm[l], k_norm[l], w_fourier[l],
                                        w_out[l], not last)
        x = x + g1 * mix_lat
        f_lat = modulate(rms_norm(x, norm_ffn[l]), sh2, sc2).reshape(B * S, D)
        if last:
            y = grouped_moe(f_lat, router_w, router_bias, w_gate[l], w_up[l], w_down[l])
            x = x + g2 * y.reshape(B, S, D)
        else:
            ctx = ctx + cg1 * mix_ctx
            f_ctx = modulate(rms_norm(ctx, norm_ffn[l]), csh2, csc2).reshape(B * n_ctx, D)
            y = grouped_moe(jnp.concatenate([f_ctx, f_lat], axis=0), router_w, router_bias,
                            w_gate[l], w_up[l], w_down[l])
            ctx = ctx + cg2 * y[:B * n_ctx].reshape(B, n_ctx, D)
            x = x + g2 * y[B * n_ctx:].reshape(B, S, D)
    return x
```

```python
import functools
import math

import numpy as np
import jax
import jax.numpy as jnp
from jax import lax
from jax.experimental import pallas as pl
from jax.experimental.pallas import tpu as pltpu

F32 = jnp.float32
BF16 = jnp.bfloat16
HIGHEST = lax.Precision.HIGHEST

D_MODEL = 1024
N_HEADS = 8
QK_NOPE = 64
QK_ROPE = 32
QK_DIM = QK_NOPE + QK_ROPE
V_DIM = 64
Q_LORA = 256
KV_LORA = 128
ROPE_AXIS = QK_ROPE // 2
ROPE_HALF = ROPE_AXIS // 2
ROPE_BASE = 10000.0
GRID_W = 64
F_GROUPS = 4
F_GDIM = 128
F_WIDTH = F_GROUPS * F_GDIM
MLA_WIDTH = N_HEADS * V_DIM
N_EXPERTS = 16
N_EGROUPS = 4
EPG = N_EXPERTS // N_EGROUPS
D_EXPERT = 256
EPS = 1e-6

LANES = 128
SLOT = LANES
HSLOTS = N_HEADS * SLOT
VMEM_LIMIT = 56 * 1024 * 1024

C_CQ = 0
C_CKV = C_CQ + Q_LORA
C_KR = C_CKV + KV_LORA
C_KRS = C_KR + SLOT
C_F = C_KRS + SLOT
IN_COLS = C_F + F_WIDTH


def _cparams(sem):
    return pltpu.CompilerParams(dimension_semantics=sem, vmem_limit_bytes=VMEM_LIMIT)


def _full(shape):
    n = len(shape)
    return pl.BlockSpec(shape, lambda *_: (0,) * n)


def _ada_kernel(c_ref, w_ref, b_ref, o_ref):
    c = c_ref[...]
    s = c * jax.nn.sigmoid(c)
    o_ref[0] = jnp.dot(s, w_ref[0], preferred_element_type=F32, precision=HIGHEST) + b_ref[0]


def _ada_call(c8, w_ada, b_ada):
    L, D, W = w_ada.shape
    tn = 1536
    return pl.pallas_call(
        _ada_kernel,
        grid=(L, W // tn),
        in_specs=[
            pl.BlockSpec((8, D), lambda l, j: (0, 0)),
            pl.BlockSpec((1, D, tn), lambda l, j: (l, 0, j)),
            pl.BlockSpec((1, 1, tn), lambda l, j: (l, 0, j)),
        ],
        out_specs=pl.BlockSpec((1, 8, tn), lambda l, j: (l, 0, j)),
        out_shape=jax.ShapeDtypeStruct((L, 8, W), F32),
        compiler_params=_cparams(("parallel", "parallel")),
        name="adaln_mod",
    )(c8, w_ada, b_ada.reshape(L, 1, W))


def _fmat_kernel(cc_ref, sc_ref, wf_ref, o_ref):
    o_ref[...] = jnp.zeros(o_ref.shape, o_ref.dtype)
    for g in range(F_GROUPS):
        wf = wf_ref[0, g]
        mr = jnp.dot(cc_ref[...], wf, preferred_element_type=F32, precision=HIGHEST)
        mi = jnp.dot(sc_ref[...], wf, preferred_element_type=F32, precision=HIGHEST)
        r0 = g * F_GDIM
        o_ref[0, r0:r0 + F_GDIM, r0:r0 + F_GDIM] = mr.astype(o_ref.dtype)
        o_ref[0, r0:r0 + F_GDIM, F_WIDTH + r0:F_WIDTH + r0 + F_GDIM] = mi.astype(o_ref.dtype)


def _fmat_call(w_fourier):
    L = w_fourier.shape[0]
    c = np.arange(F_GDIM)
    ang = 2.0 * np.pi * np.outer(c, c) / F_GDIM
    cc = jnp.asarray((np.cos(ang) / math.sqrt(F_GDIM)).astype(np.float32))
    sc = jnp.asarray((-np.sin(ang) / math.sqrt(F_GDIM)).astype(np.float32))
    return pl.pallas_call(
        _fmat_kernel,
        grid=(L,),
        in_specs=[
            _full((F_GDIM, F_GDIM)),
            _full((F_GDIM, F_GDIM)),
            pl.BlockSpec((1, F_GROUPS, F_GDIM, F_GDIM), lambda l: (l, 0, 0, 0)),
        ],
        out_specs=pl.BlockSpec((1, F_WIDTH, 2 * F_WIDTH), lambda l: (l, 0, 0)),
        out_shape=jax.ShapeDtypeStruct((L, F_WIDTH, 2 * F_WIDTH), BF16),
        compiler_params=_cparams(("parallel",)),
        name="fourier_chan_mats",
    )(cc, sc, w_fourier)


def _rms(x):
    return x * lax.rsqrt(jnp.mean(x * x, axis=-1, keepdims=True) + EPS)


def _tile8(a):
    return jnp.concatenate([a] * N_HEADS, axis=-1)


def _premix_kernel(x_ref, sh_ref, sc_ref, gn_ref, win_ref, qa_ref, wq_ref, wqs_ref, kva_ref, wk_ref, wv_ref,
                   e_ref, cos_ref, sin_ref, gq_ref, gqs_ref, gk_ref, gks_ref, mf_ref,
                   q_ref, k_ref, v_ref, wri_ref):
    x = x_ref[0]
    h = _rms(x) * gn_ref[...] * (1.0 + sc_ref[0]) + sh_ref[0]
    proj = jnp.dot(h.astype(BF16), win_ref[...], preferred_element_type=F32)
    cq = proj[:, C_CQ:C_CKV]
    ckv = proj[:, C_CKV:C_KR]
    kr = proj[:, C_KR:C_KRS]
    krs = proj[:, C_KRS:C_F]
    f = proj[:, C_F:IN_COLS]

    cos8 = _tile8(cos_ref[...])
    sin8 = _tile8(sin_ref[...])
    inv_d = 1.0 / QK_DIM

    cqn = (_rms(cq) * qa_ref[...]).astype(BF16)
    qr = jnp.dot(cqn, wq_ref[...], preferred_element_type=F32)
    qs = jnp.dot(cqn, wqs_ref[...], preferred_element_type=F32)
    ssq = jnp.dot((qr * qr).astype(BF16), e_ref[...], preferred_element_type=F32)
    rq = lax.rsqrt(ssq * inv_d + EPS)
    q = rq * (qr * (gq_ref[...] * cos8) + qs * (gqs_ref[...] * sin8))
    q_ref[0] = q.astype(q_ref.dtype)

    ckvn = (_rms(ckv) * kva_ref[...]).astype(BF16)
    kraw = jnp.dot(ckvn, wk_ref[...], preferred_element_type=F32) + _tile8(kr)
    ssk = jnp.dot((kraw * kraw).astype(BF16), e_ref[...], preferred_element_type=F32)
    rk = lax.rsqrt(ssk * inv_d + EPS)
    k = rk * (kraw * (gk_ref[...] * cos8) + _tile8(krs) * (gks_ref[...] * sin8))
    k_ref[0] = k.astype(k_ref.dtype)
    v_ref[0] = jnp.dot(ckvn, wv_ref[...], preferred_element_type=F32).astype(v_ref.dtype)

    w = jnp.dot(f.astype(BF16), mf_ref[...], preferred_element_type=F32)
    wri_ref[0, 0] = w[:, :F_WIDTH].astype(wri_ref.dtype)
    wri_ref[0, 1] = w[:, F_WIDTH:].astype(wri_ref.dtype)


def _premix_call(x, sh, sc, lw, cos_t, sin_t, tm):
    B, N, D = x.shape
    row = lambda b, i: (b, i, 0)
    vec = lambda b, i: (b, 0, 0)
    in_specs = [
        pl.BlockSpec((1, tm, D), row),
        pl.BlockSpec((1, 1, D), vec),
        pl.BlockSpec((1, 1, D), vec),
        _full((1, D)),
        _full((D, IN_COLS)),
        _full((1, Q_LORA)),
        _full((Q_LORA, HSLOTS)),
        _full((Q_LORA, HSLOTS)),
        _full((1, KV_LORA)),
        _full((KV_LORA, HSLOTS)),
        _full((KV_LORA, MLA_WIDTH)),
        _full((HSLOTS, HSLOTS)),
        pl.BlockSpec((tm, SLOT), lambda b, i: (i, 0)),
        pl.BlockSpec((tm, SLOT), lambda b, i: (i, 0)),
        _full((1, HSLOTS)),
        _full((1, HSLOTS)),
        _full((1, HSLOTS)),
        _full((1, HSLOTS)),
        _full((F_WIDTH, 2 * F_WIDTH)),
    ]
    out_specs = [
        pl.BlockSpec((1, tm, HSLOTS), row),
        pl.BlockSpec((1, tm, HSLOTS), row),
        pl.BlockSpec((1, tm, MLA_WIDTH), row),
        pl.BlockSpec((1, 2, tm, F_WIDTH), lambda b, i: (b, 0, i, 0)),
    ]
    out_shape = [
        jax.ShapeDtypeStruct((B, N, HSLOTS), BF16),
        jax.ShapeDtypeStruct((B, N, HSLOTS), BF16),
        jax.ShapeDtypeStruct((B, N, MLA_WIDTH), BF16),
        jax.ShapeDtypeStruct((B, 2, N, F_WIDTH), BF16),
    ]
    return pl.pallas_call(
        _premix_kernel,
        grid=(B, N // tm),
        in_specs=in_specs,
        out_specs=out_specs,
        out_shape=out_shape,
        compiler_params=_cparams(("parallel", "parallel")),
        name="premix_proj",
    )(x, sh, sc, lw["gmix"], lw["win"], lw["qa"], lw["wq"], lw["wqs"], lw["kva"], lw["wk"], lw["wv"],
      lw["esum"], cos_t, sin_t, lw["gq"], lw["gqs"], lw["gk"], lw["gks"], lw["mf"])


def _attn_kernel(q_ref, k_ref, v_ref, o_ref, *, tk, n_kt):
    tq = q_ref.shape[1]
    outs = []
    for hh in range(2):
        q = q_ref[0, :, hh * SLOT:(hh + 1) * SLOT]

        def body(j, carry, hh=hh, q=q):
            m, l, acc = carry
            off = pl.multiple_of(j * tk, tk)
            kj = k_ref[0, pl.ds(off, tk), hh * SLOT:(hh + 1) * SLOT]
            vj = v_ref[0, pl.ds(off, tk), :]
            s = lax.dot_general(q, kj, (((1,), (1,)), ((), ())), preferred_element_type=F32)
            m_new = jnp.maximum(m, jnp.max(s, axis=-1, keepdims=True))
            alpha = jnp.exp(m - m_new)
            p = jnp.exp(s - m_new)
            l_new = alpha * l + jnp.sum(p, axis=-1, keepdims=True)
            acc_new = alpha * acc + jnp.dot(p.astype(BF16), vj, preferred_element_type=F32)
            return m_new, l_new, acc_new

        init = (jnp.full((tq, 1), -1e30, F32), jnp.zeros((tq, 1), F32), jnp.zeros((tq, SLOT), F32))
        m, l, acc = lax.fori_loop(0, n_kt, body, init)
        outs.append(acc / l)
    lane = lax.broadcasted_iota(jnp.int32, (tq, SLOT), 1)
    o_ref[0] = jnp.where(lane < V_DIM, outs[0], outs[1]).astype(o_ref.dtype)


def _pick_tk(nk):
    for tk in (1280, 1024, 512, 256):
        if nk % tk == 0:
            return tk
    raise ValueError(f"unsupported key count {nk}")


def _attn_call(q, k, v, tq):
    B, Nq, _ = q.shape
    Nk = k.shape[1]
    tk = _pick_tk(Nk)
    kern = functools.partial(_attn_kernel, tk=tk, n_kt=Nk // tk)
    return pl.pallas_call(
        kern,
        grid=(B, N_HEADS // 2, Nq // tq),
        in_specs=[
            pl.BlockSpec((1, tq, 2 * SLOT), lambda b, h, i: (b, i, h)),
            pl.BlockSpec((1, Nk, 2 * SLOT), lambda b, h, i: (b, 0, h)),
            pl.BlockSpec((1, Nk, SLOT), lambda b, h, i: (b, 0, h)),
        ],
        out_specs=pl.BlockSpec((1, tq, SLOT), lambda b, h, i: (b, i, h)),
        out_shape=jax.ShapeDtypeStruct((B, Nq, MLA_WIDTH), BF16),
        compiler_params=_cparams(("parallel", "parallel", "arbitrary")),
        name="mla_attention",
    )(q, k, v)


def _fft1_kernel(f1_ref, tr_ref, ti_ref, w_ref, y_ref, *, n1, n2_per):
    y = jnp.dot(f1_ref[...], w_ref[0], preferred_element_type=F32)
    yr = y[:n1]
    yi = y[n1:]
    for t in range(n2_per):
        tr = tr_ref[:, t * LANES:(t + 1) * LANES]
        ti = ti_ref[:, t * LANES:(t + 1) * LANES]
        tr4 = jnp.concatenate([tr] * F_GROUPS, axis=-1)
        ti4 = jnp.concatenate([ti] * F_GROUPS, axis=-1)
        a = yr[:, t * F_WIDTH:(t + 1) * F_WIDTH]
        b = yi[:, t * F_WIDTH:(t + 1) * F_WIDTH]
        y_ref[0, :n1, t * F_WIDTH:(t + 1) * F_WIDTH] = (a * tr4 - b * ti4).astype(y_ref.dtype)
        y_ref[0, n1:, t * F_WIDTH:(t + 1) * F_WIDTH] = (a * ti4 + b * tr4).astype(y_ref.dtype)


def _fft2_kernel(c2_ref, s2_ref, y_ref, o_ref, *, kc):
    for i in range(kc):
        re = jnp.dot(c2_ref[...], y_ref[0, 0, i], preferred_element_type=F32)
        im = jnp.dot(s2_ref[...], y_ref[0, 1, i], preferred_element_type=F32)
        o_ref[0, :, i * F_WIDTH:(i + 1) * F_WIDTH] = (re + im).astype(o_ref.dtype)


def _fft_consts(n):
    n2 = LANES
    n1 = n // n2
    a1 = 2.0 * np.pi * np.outer(np.arange(n1), np.arange(n1)) / n1
    c1, s1 = np.cos(a1), np.sin(a1)
    f1 = np.block([[c1, s1], [-s1, c1]]).astype(np.float32)
    at = 2.0 * np.pi * np.outer(np.arange(n1), np.arange(n2)) / n
    tr = np.repeat(np.cos(at), LANES, axis=1).astype(np.float32)
    ti = np.repeat(-np.sin(at), LANES, axis=1).astype(np.float32)
    a2 = 2.0 * np.pi * np.outer(np.arange(n2), np.arange(n2)) / n2
    scale = 1.0 / math.sqrt(n)
    c2 = (np.cos(a2) * scale).astype(np.float32)
    s2 = (np.sin(a2) * scale).astype(np.float32)
    return n1, n2, f1, tr, ti, c2, s2


def _fft_call(wri):
    B, _, N, _ = wri.shape
    n1, n2, f1, tr, ti, c2, s2 = _fft_consts(N)
    cols = n2 * F_WIDTH
    n2_per = 4
    tc = n2_per * F_WIDTH
    y = pl.pallas_call(
        functools.partial(_fft1_kernel, n1=n1, n2_per=n2_per),
        grid=(B, cols // tc),
        in_specs=[
            _full((2 * n1, 2 * n1)),
            pl.BlockSpec((n1, n2_per * LANES), lambda b, j: (0, j)),
            pl.BlockSpec((n1, n2_per * LANES), lambda b, j: (0, j)),
            pl.BlockSpec((1, 2 * n1, tc), lambda b, j: (b, 0, j)),
        ],
        out_specs=pl.BlockSpec((1, 2 * n1, tc), lambda b, j: (b, 0, j)),
        out_shape=jax.ShapeDtypeStruct((B, 2 * n1, cols), BF16),
        compiler_params=_cparams(("parallel", "parallel")),
        name="fft_stage1",
    )(jnp.asarray(f1, BF16), jnp.asarray(tr), jnp.asarray(ti), wri.reshape(B, 2 * n1, cols))
    kc = min(8, n1)
    out = pl.pallas_call(
        functools.partial(_fft2_kernel, kc=kc),
        grid=(B, n1 // kc),
        in_specs=[
            _full((n2, n2)),
            _full((n2, n2)),
            pl.BlockSpec((1, 2, kc, n2, F_WIDTH), lambda b, j: (b, 0, j, 0, 0)),
        ],
        out_specs=pl.BlockSpec((1, n2, kc * F_WIDTH), lambda b, j: (b, 0, j)),
        out_shape=jax.ShapeDtypeStruct((B, n2, n1 * F_WIDTH), BF16),
        compiler_params=_cparams(("parallel", "parallel")),
        name="fft_stage2",
    )(jnp.asarray(c2, BF16), jnp.asarray(s2, BF16), y.reshape(B, 2, n1, n2, F_WIDTH))
    return out.reshape(B, N, F_WIDTH)


def _dft_small_kernel(c_ref, s_ref, w_ref, o_ref):
    re = jnp.dot(c_ref[...], w_ref[0, 0], preferred_element_type=F32)
    im = jnp.dot(s_ref[...], w_ref[0, 1], preferred_element_type=F32)
    o_ref[0] = (re + im).astype(o_ref.dtype)


def _dft_small_call(wri):
    B, _, N, _ = wri.shape
    a = 2.0 * np.pi * np.outer(np.arange(N), np.arange(N)) / N
    c = (np.cos(a) / math.sqrt(N)).astype(np.float32)
    s = (np.sin(a) / math.sqrt(N)).astype(np.float32)
    return pl.pallas_call(
        _dft_small_kernel,
        grid=(B,),
        in_specs=[_full((N, N)), _full((N, N)), pl.BlockSpec((1, 2, N, F_WIDTH), lambda b: (b, 0, 0, 0))],
        out_specs=pl.BlockSpec((1, N, F_WIDTH), lambda b: (b, 0, 0)),
        out_shape=jax.ShapeDtypeStruct((B, N, F_WIDTH), BF16),
        compiler_params=_cparams(("parallel",)),
        name="dft_small",
    )(jnp.asarray(c, BF16), jnp.asarray(s, BF16), wri)


def _router_rows(logits, bias):
    t = logits.shape[1]
    s = jax.nn.sigmoid(logits)
    b = s + bias
    srow = [s[e:e + 1, :] for e in range(N_EXPERTS)]
    brow = [b[e:e + 1, :] for e in range(N_EXPERTS)]
    gscore = []
    for g in range(N_EGROUPS):
        b0, b1, b2, b3 = brow[EPG * g:EPG * g + EPG]
        m1, n1 = jnp.maximum(b0, b1), jnp.minimum(b0, b1)
        m2, n2 = jnp.maximum(b2, b3), jnp.minimum(b2, b3)
        top1 = jnp.maximum(m1, m2)
        top2 = jnp.maximum(jnp.minimum(m1, m2), jnp.maximum(n1, n2))
        gscore.append(top1 + top2)
    best = gscore[0]
    gi = jnp.zeros((1, t), jnp.int32)
    for g in range(1, N_EGROUPS):
        take = gscore[g] > best
        gi = jnp.where(take, g, gi)
        best = jnp.where(take, gscore[g], best)
    ib, isc = [], []
    for i in range(EPG):
        vb, vs = brow[i], srow[i]
        for g in range(1, N_EGROUPS):
            sel = gi == g
            vb = jnp.where(sel, brow[EPG * g + i], vb)
            vs = jnp.where(sel, srow[EPG * g + i], vs)
        ib.append(vb)
        isc.append(vs)
    w = []
    for i in range(EPG):
        rank = jnp.zeros((1, t), jnp.int32)
        for j in range(EPG):
            if j == i:
                continue
            ahead = (ib[j] > ib[i]) | ((ib[j] == ib[i]) & (j < i))
            rank = rank + ahead.astype(jnp.int32)
        w.append(jnp.where(rank < 2, isc[i], 0.0))
    tot = (w[0] + w[1]) + (w[2] + w[3])
    inv = 1.0 / tot
    eidx = lax.broadcasted_iota(jnp.int32, (N_EXPERTS, t), 0)
    in_group = jnp.right_shift(eidx, 2) == gi
    dense = jnp.zeros((N_EXPERTS, t), F32)
    for i in range(EPG):
        dense = jnp.where(in_group & (jnp.bitwise_and(eidx, EPG - 1) == i), w[i] * inv, dense)
    return dense


def _postmix_kernel(o_ref, z_ref, x_ref, g1_ref, sh_ref, sc_ref, gn_ref, wo_ref, rw_ref, rb_ref,
                    xo_ref, f_ref, wd_ref):
    mix = jnp.dot(o_ref[0], wo_ref[:MLA_WIDTH, :], preferred_element_type=F32)
    mix = mix + jnp.dot(z_ref[0], wo_ref[MLA_WIDTH:, :], preferred_element_type=F32)
    xn = x_ref[0] + g1_ref[0] * mix
    xo_ref[0] = xn
    f = _rms(xn) * gn_ref[...] * (1.0 + sc_ref[0]) + sh_ref[0]
    f_ref[0] = f.astype(f_ref.dtype)
    logits = lax.dot_general(rw_ref[...], f, (((1,), (1,)), ((), ())), preferred_element_type=F32,
                             precision=HIGHEST)
    wd_ref[0] = _router_rows(logits, rb_ref[...])


def _postmix_call(o, z, x, g1, sh, sc, lw, rwt, rb, tm):
    B, N, D = x.shape
    row = lambda b, i: (b, i, 0)
    vec = lambda b, i: (b, 0, 0)
    return pl.pallas_call(
        _postmix_kernel,
        grid=(B, N // tm),
        in_specs=[
            pl.BlockSpec((1, tm, MLA_WIDTH), row),
            pl.BlockSpec((1, tm, F_WIDTH), row),
            pl.BlockSpec((1, tm, D), row),
            pl.BlockSpec((1, 1, D), vec),
            pl.BlockSpec((1, 1, D), vec),
            pl.BlockSpec((1, 1, D), vec),
            _full((1, D)),
            _full((D, D)),
            _full((N_EXPERTS, D)),
            _full((N_EXPERTS, 1)),
        ],
        out_specs=[
            pl.BlockSpec((1, tm, D), row),
            pl.BlockSpec((1, tm, D), row),
            pl.BlockSpec((1, N_EXPERTS, tm), lambda b, i: (b, 0, i)),
        ],
        out_shape=[
            jax.ShapeDtypeStruct((B, N, D), F32),
            jax.ShapeDtypeStruct((B, N, D), BF16),
            jax.ShapeDtypeStruct((B, N_EXPERTS, N), F32),
        ],
        compiler_params=_cparams(("parallel", "parallel")),
        name="postmix_router",
    )(o, z, x, g1, sh, sc, lw["gffn"], lw["wout"], rwt, rb)


def _moe_kernel(f_ref, w2_ref, x_ref, g2_ref, ex_ref, wg_ref, wu_ref, wd_ref, xo_ref, acc_ref):
    g = pl.program_id(2)

    @pl.when(g == 0)
    def _():
        acc_ref[...] = jnp.zeros(acc_ref.shape, acc_ref.dtype)

    f = f_ref[0]
    hg = jnp.dot(f, wg_ref[0], preferred_element_type=F32)
    hu = jnp.dot(f, wu_ref[0], preferred_element_type=F32)
    wexp = jnp.dot(w2_ref[0], ex_ref[0], preferred_element_type=F32)
    h = (hg * jax.nn.sigmoid(hg)) * hu * wexp
    acc_ref[...] += jnp.dot(h.astype(BF16), wd_ref[0], preferred_element_type=F32)

    @pl.when(g == N_EGROUPS - 1)
    def _():
        xo_ref[0] = x_ref[0] + g2_ref[0] * acc_ref[...]


def _moe_call(f, w2, x, g2, lw, tm):
    B, N, D = x.shape
    GW = EPG * D_EXPERT
    row = lambda b, i, g: (b, i, 0)
    return pl.pallas_call(
        _moe_kernel,
        grid=(B, N // tm, N_EGROUPS),
        in_specs=[
            pl.BlockSpec((1, tm, D), row),
            pl.BlockSpec((1, tm, 2 * N_EXPERTS), row),
            pl.BlockSpec((1, tm, D), row),
            pl.BlockSpec((1, 1, D), lambda b, i, g: (b, 0, 0)),
            pl.BlockSpec((1, 2 * N_EXPERTS, GW), lambda b, i, g: (g, 0, 0)),
            pl.BlockSpec((1, D, GW), lambda b, i, g: (g, 0, 0)),
            pl.BlockSpec((1, D, GW), lambda b, i, g: (g, 0, 0)),
            pl.BlockSpec((1, GW, D), lambda b, i, g: (g, 0, 0)),
        ],
        out_specs=pl.BlockSpec((1, tm, D), row),
        out_shape=jax.ShapeDtypeStruct((B, N, D), F32),
        scratch_shapes=[pltpu.VMEM((tm, D), F32)],
        compiler_params=_cparams(("parallel", "parallel", "arbitrary")),
        name="moe_experts",
    )(f, w2, x, g2, lw["expand"], lw["wgate"], lw["wup"], lw["wdown"])


def _rope_partner():
    p = np.arange(SLOT)
    for j in range(ROPE_HALF):
        for base in (QK_NOPE, QK_NOPE + ROPE_AXIS):
            p[base + j] = base + ROPE_HALF + j
            p[base + ROPE_HALF + j] = base + j
    return p


def _rope_tables(n):
    rows = n // GRID_W
    row = jnp.repeat(jnp.arange(rows, dtype=F32), GRID_W)
    col = jnp.tile(jnp.arange(GRID_W, dtype=F32), rows)
    freqs = ROPE_BASE ** (-jnp.arange(ROPE_HALF, dtype=F32) / ROPE_HALF)
    ar, ac = row[:, None] * freqs, col[:, None] * freqs
    one = jnp.ones((n, QK_NOPE), F32)
    zero = jnp.zeros((n, QK_NOPE), F32)
    pad1 = jnp.ones((n, SLOT - QK_DIM), F32)
    pad0 = jnp.zeros((n, SLOT - QK_DIM), F32)
    cos_t = jnp.concatenate([one, jnp.cos(ar), jnp.cos(ar), jnp.cos(ac), jnp.cos(ac), pad1], axis=1)
    sin_t = jnp.concatenate([zero, -jnp.sin(ar), jnp.sin(ar), -jnp.sin(ac), jnp.sin(ac), pad0], axis=1)
    return cos_t, sin_t


def _slot_cols(w, width):
    kdim = w.shape[0]
    w = w.reshape(kdim, N_HEADS, width)
    return jnp.pad(w, ((0, 0), (0, 0), (0, SLOT - width))).reshape(kdim, HSLOTS)


def _layer_weights(l, p, mf_all):
    partner = _rope_partner()
    w_in = p["w_in"][l]
    kr_cols = w_in[:, Q_LORA + KV_LORA:Q_LORA + KV_LORA + QK_ROPE]
    kr_slot = jnp.pad(kr_cols, ((0, 0), (QK_NOPE, SLOT - QK_DIM)))
    kr_swap = jnp.where((np.arange(SLOT) >= QK_NOPE) & (np.arange(SLOT) < QK_DIM), kr_slot[:, partner], 0.0)
    win = jnp.concatenate([w_in[:, :Q_LORA + KV_LORA], kr_slot, kr_swap, w_in[:, Q_LORA + KV_LORA + QK_ROPE:]], axis=1)

    rope_lane = (np.arange(SLOT) >= QK_NOPE) & (np.arange(SLOT) < QK_DIM)
    wq = _slot_cols(p["w_q_b"][l], QK_DIM)
    wq3 = wq.reshape(Q_LORA, N_HEADS, SLOT)
    wqs = jnp.where(rope_lane, wq3[:, :, partner], 0.0).reshape(Q_LORA, HSLOTS)

    wkv = p["w_kv_b"][l].reshape(KV_LORA, N_HEADS, QK_NOPE + V_DIM)
    wk = jnp.pad(wkv[:, :, :QK_NOPE], ((0, 0), (0, 0), (0, SLOT - QK_NOPE))).reshape(KV_LORA, HSLOTS)
    wv = wkv[:, :, QK_NOPE:].reshape(KV_LORA, MLA_WIDTH)

    def gains(g, scale):
        gp = jnp.pad(g, (0, SLOT - QK_DIM))
        gs = jnp.where(rope_lane, gp[partner], 0.0)
        return (jnp.tile(gp, N_HEADS) * scale)[None, :], (jnp.tile(gs, N_HEADS) * scale)[None, :]

    gq, gqs = gains(p["q_norm"][l], QK_DIM ** -0.5)
    gk, gks = gains(p["k_norm"][l], 1.0)

    def group_cols(w):
        return w.reshape(N_EGROUPS, EPG, D_MODEL, D_EXPERT).transpose(0, 2, 1, 3).reshape(
            N_EGROUPS, D_MODEL, EPG * D_EXPERT).astype(BF16)

    return {
        "gmix": p["norm_mix"][l][None, :],
        "gffn": p["norm_ffn"][l][None, :],
        "win": win.astype(BF16),
        "qa": p["q_a_norm"][l][None, :],
        "wq": wq.astype(BF16),
        "wqs": wqs.astype(BF16),
        "kva": p["kv_a_norm"][l][None, :],
        "wk": wk.astype(BF16),
        "wv": wv.astype(BF16),
        "gq": gq, "gqs": gqs, "gk": gk, "gks": gks,
        "mf": mf_all[l],
        "wout": p["w_out"][l].astype(BF16),
        "wgate": group_cols(p["w_gate"][l]),
        "wup": group_cols(p["w_up"][l]),
        "wdown": p["w_down"][l].reshape(N_EGROUPS, EPG * D_EXPERT, D_MODEL).astype(BF16),
    }


def _const_tables():
    lane_head = np.arange(HSLOTS) // SLOT
    esum = (lane_head[:, None] == lane_head[None, :]).astype(np.float32)
    ex = np.zeros((N_EGROUPS, 2 * N_EXPERTS, EPG * D_EXPERT), np.float32)
    for g in range(N_EGROUPS):
        for i in range(EPG):
            ex[g, EPG * g + i, i * D_EXPERT:(i + 1) * D_EXPERT] = 1.0
            ex[g, N_EXPERTS + EPG * g + i, i * D_EXPERT:(i + 1) * D_EXPERT] = 1.0
    return jnp.asarray(esum, BF16), jnp.asarray(ex, BF16)


def _split_hi_lo(w):
    wt = jnp.swapaxes(w, 1, 2)
    hi = wt.astype(BF16)
    lo = (wt - hi.astype(F32)).astype(BF16)
    return jnp.concatenate([hi, lo], axis=-1)


def _mod6(mod_l, rows, B):
    m = mod_l[jnp.asarray(rows)]
    return [m[:, i * D_MODEL:(i + 1) * D_MODEL][:, None, :] for i in range(6)]


def kernel(x, c, ctx, c_ctx, w_ada, b_ada, norm_mix, norm_ffn, w_in, q_a_norm, w_q_b, kv_a_norm, w_kv_b, q_norm,
           k_norm, w_fourier, w_out, router_w, router_bias, w_gate, w_up, w_down):
    B, S, D = x.shape
    n_ctx = ctx.shape[1]
    depth = w_ada.shape[0]
    p = dict(w_in=w_in, q_a_norm=q_a_norm, w_q_b=w_q_b, kv_a_norm=kv_a_norm, w_kv_b=w_kv_b, q_norm=q_norm,
             k_norm=k_norm, norm_mix=norm_mix, norm_ffn=norm_ffn, w_out=w_out, w_gate=w_gate, w_up=w_up,
             w_down=w_down)

    c8 = jnp.concatenate([c, c_ctx[None, :], jnp.zeros((8 - B - 1, D), F32)], axis=0)
    mod = _ada_call(c8, w_ada, b_ada)
    mf_all = _fmat_call(w_fourier)
    esum, expand = _const_tables()
    cos_l, sin_l = _rope_tables(S)
    cos_c = jnp.ones((n_ctx, SLOT), F32)
    sin_c = jnp.zeros((n_ctx, SLOT), F32)
    rwt = router_w.T
    rb = router_bias[:, None]

    tm_lat = 512
    tm_ctx = min(256, n_ctx)
    for l in range(depth):
        last = l == depth - 1
        lw = _layer_weights(l, p, mf_all)
        lw["esum"] = esum
        lw["expand"] = expand
        sh1, sc1, g1, sh2, sc2, g2 = _mod6(mod[l], list(range(B)), B)
        csh1, csc1, cg1, csh2, csc2, cg2 = _mod6(mod[l], [B] * B, B)

        q_c, k_c, v_c, wri_c = _premix_call(ctx, csh1, csc1, lw, cos_c, sin_c, tm_ctx)
        q_l, k_l, v_l, wri_l = _premix_call(x, sh1, sc1, lw, cos_l, sin_l, tm_lat)
        k_all = jnp.concatenate([k_c, k_l], axis=1)
        v_all = jnp.concatenate([v_c, v_l], axis=1)
        o_l = _attn_call(q_l, k_all, v_all, 512)
        z_l = _fft_call(wri_l)
        x_mid, f_l, wd_l = _postmix_call(o_l, z_l, x, g1, sh2, sc2, lw, rwt, rb, tm_lat)
        x = _moe_call(f_l, _split_hi_lo(wd_l), x_mid, g2, lw, tm_lat)
        if not last:
            o_c = _attn_call(q_c, k_c, v_c, tm_ctx)
            z_c = _dft_small_call(wri_c)
            c_mid, f_c, wd_c = _postmix_call(o_c, z_c, ctx, cg1, csh2, csc2, lw, rwt, rb, tm_ctx)
            ctx = _moe_call(f_c, _split_hi_lo(wd_c), c_mid, cg2, lw, tm_ctx)
    return x
```

```python
import functools
import math

import numpy as np
import jax
import jax.numpy as jnp
from jax import lax
from jax.experimental import pallas as pl
from jax.experimental.pallas import tpu as pltpu

F32 = jnp.float32
BF16 = jnp.bfloat16
HIGHEST = lax.Precision.HIGHEST

D_MODEL = 1024
N_HEADS = 8
QK_NOPE = 64
QK_ROPE = 32
QK_DIM = QK_NOPE + QK_ROPE
V_DIM = 64
Q_LORA = 256
KV_LORA = 128
ROPE_AXIS = QK_ROPE // 2
ROPE_HALF = ROPE_AXIS // 2
ROPE_BASE = 10000.0
GRID_W = 64
F_GROUPS = 4
F_GDIM = 128
F_WIDTH = F_GROUPS * F_GDIM
MLA_WIDTH = N_HEADS * V_DIM
N_EXPERTS = 16
N_EGROUPS = 4
EPG = N_EXPERTS // N_EGROUPS
D_EXPERT = 256
EPS = 1e-6

LANES = 128
SLOT = LANES
HSLOTS = N_HEADS * SLOT
VMEM_LIMIT = 56 * 1024 * 1024

C_CQ = 0
C_CKV = C_CQ + Q_LORA
C_KR = C_CKV + KV_LORA
C_KRS = C_KR + SLOT
C_F = C_KRS + SLOT
IN_COLS = C_F + F_WIDTH


def _cparams(sem):
    return pltpu.CompilerParams(dimension_semantics=sem, vmem_limit_bytes=VMEM_LIMIT)


def _full(shape):
    n = len(shape)
    return pl.BlockSpec(shape, lambda *_: (0,) * n)


def _ada_kernel(c_ref, w_ref, b_ref, o_ref):
    c = c_ref[...]
    s = c * jax.nn.sigmoid(c)
    o_ref[0] = jnp.dot(s, w_ref[0], preferred_element_type=F32, precision=HIGHEST) + b_ref[0]


def _ada_call(c8, w_ada, b_ada):
    L, D, W = w_ada.shape
    tn = 1536
    return pl.pallas_call(
        _ada_kernel,
        grid=(L, W // tn),
        in_specs=[
            pl.BlockSpec((8, D), lambda l, j: (0, 0)),
            pl.BlockSpec((1, D, tn), lambda l, j: (l, 0, j)),
            pl.BlockSpec((1, 1, tn), lambda l, j: (l, 0, j)),
        ],
        out_specs=pl.BlockSpec((1, 8, tn), lambda l, j: (l, 0, j)),
        out_shape=jax.ShapeDtypeStruct((L, 8, W), F32),
        compiler_params=_cparams(("parallel", "parallel")),
        name="adaln_mod",
    )(c8, w_ada, b_ada.reshape(L, 1, W))


def _fmat_kernel(cc_ref, sc_ref, wf_ref, o_ref):
    o_ref[...] = jnp.zeros(o_ref.shape, o_ref.dtype)
    for g in range(F_GROUPS):
        wf = wf_ref[0, g]
        mr = jnp.dot(cc_ref[...], wf, preferred_element_type=F32, precision=HIGHEST)
        mi = jnp.dot(sc_ref[...], wf, preferred_element_type=F32, precision=HIGHEST)
        r0 = g * F_GDIM
        o_ref[0, r0:r0 + F_GDIM, r0:r0 + F_GDIM] = mr.astype(o_ref.dtype)
        o_ref[0, r0:r0 + F_GDIM, F_WIDTH + r0:F_WIDTH + r0 + F_GDIM] = mi.astype(o_ref.dtype)


def _fmat_call(w_fourier):
    L = w_fourier.shape[0]
    c = np.arange(F_GDIM)
    ang = 2.0 * np.pi * np.outer(c, c) / F_GDIM
    cc = jnp.asarray((np.cos(ang) / math.sqrt(F_GDIM)).astype(np.float32))
    sc = jnp.asarray((-np.sin(ang) / math.sqrt(F_GDIM)).astype(np.float32))
    return pl.pallas_call(
        _fmat_kernel,
        grid=(L,),
        in_specs=[
            _full((F_GDIM, F_GDIM)),
            _full((F_GDIM, F_GDIM)),
            pl.BlockSpec((1, F_GROUPS, F_GDIM, F_GDIM), lambda l: (l, 0, 0, 0)),
        ],
        out_specs=pl.BlockSpec((1, F_WIDTH, 2 * F_WIDTH), lambda l: (l, 0, 0)),
        out_shape=jax.ShapeDtypeStruct((L, F_WIDTH, 2 * F_WIDTH), BF16),
        compiler_params=_cparams(("parallel",)),
        name="fourier_chan_mats",
    )(cc, sc, w_fourier)


def _rms(x):
    return x * lax.rsqrt(jnp.mean(x * x, axis=-1, keepdims=True) + EPS)


def _tile8(a):
    return jnp.concatenate([a] * N_HEADS, axis=-1)


def _premix_kernel(x_ref, sh_ref, sc_ref, gn_ref, win_ref, qa_ref, wq_ref, wqs_ref, kva_ref, wk_ref, wv_ref,
                   e_ref, cos_ref, sin_ref, gq_ref, gqs_ref, gk_ref, gks_ref, mf_ref,
                   q_ref, k_ref, v_ref, wri_ref):
    x = x_ref[0]
    h = _rms(x) * gn_ref[...] * (1.0 + sc_ref[0]) + sh_ref[0]
    proj = jnp.dot(h.astype(BF16), win_ref[...], preferred_element_type=F32)
    cq = proj[:, C_CQ:C_CKV]
    ckv = proj[:, C_CKV:C_KR]
    kr = proj[:, C_KR:C_KRS]
    krs = proj[:, C_KRS:C_F]
    f = proj[:, C_F:IN_COLS]

    cos8 = _tile8(cos_ref[...])
    sin8 = _tile8(sin_ref[...])
    inv_d = 1.0 / QK_DIM

    cqn = (_rms(cq) * qa_ref[...]).astype(BF16)
    qr = jnp.dot(cqn, wq_ref[...], preferred_element_type=F32)
    qs = jnp.dot(cqn, wqs_ref[...], preferred_element_type=F32)
    ssq = jnp.dot((qr * qr).astype(BF16), e_ref[...], preferred_element_type=F32)
    rq = lax.rsqrt(ssq * inv_d + EPS)
    q = rq * (qr * (gq_ref[...] * cos8) + qs * (gqs_ref[...] * sin8))
    q_ref[0] = q.astype(q_ref.dtype)

    ckvn = (_rms(ckv) * kva_ref[...]).astype(BF16)
    kraw = jnp.dot(ckvn, wk_ref[...], preferred_element_type=F32) + _tile8(kr)
    ssk = jnp.dot((kraw * kraw).astype(BF16), e_ref[...], preferred_element_type=F32)
    rk = lax.rsqrt(ssk * inv_d + EPS)
    k = rk * (kraw * (gk_ref[...] * cos8) + _tile8(krs) * (gks_ref[...] * sin8))
    k_ref[0] = k.astype(k_ref.dtype)
    v_ref[0] = jnp.dot(ckvn, wv_ref[...], preferred_element_type=F32).astype(v_ref.dtype)

    w = jnp.dot(f.astype(BF16), mf_ref[...], preferred_element_type=F32)
    wri_ref[0, 0] = w[:, :F_WIDTH].astype(wri_ref.dtype)
    wri_ref[0, 1] = w[:, F_WIDTH:].astype(wri_ref.dtype)


def _premix_call(x, sh, sc, lw, cos_t, sin_t, tm):
    B, N, D = x.shape
    row = lambda b, i: (b, i, 0)
    vec = lambda b, i: (b, 0, 0)
    in_specs = [
        pl.BlockSpec((1, tm, D), row),
        pl.BlockSpec((1, 1, D), vec),
        pl.BlockSpec((1, 1, D), vec),
        _full((1, D)),
        _full((D, IN_COLS)),
        _full((1, Q_LORA)),
        _full((Q_LORA, HSLOTS)),
        _full((Q_LORA, HSLOTS)),
        _full((1, KV_LORA)),
        _full((KV_LORA, HSLOTS)),
        _full((KV_LORA, MLA_WIDTH)),
        _full((HSLOTS, HSLOTS)),
        pl.BlockSpec((tm, SLOT), lambda b, i: (i, 0)),
        pl.BlockSpec((tm, SLOT), lambda b, i: (i, 0)),
        _full((1, HSLOTS)),
        _full((1, HSLOTS)),
        _full((1, HSLOTS)),
        _full((1, HSLOTS)),
        _full((F_WIDTH, 2 * F_WIDTH)),
    ]
    out_specs = [
        pl.BlockSpec((1, tm, HSLOTS), row),
        pl.BlockSpec((1, tm, HSLOTS), row),
        pl.BlockSpec((1, tm, MLA_WIDTH), row),
        pl.BlockSpec((1, 2, tm, F_WIDTH), lambda b, i: (b, 0, i, 0)),
    ]
    out_shape = [
        jax.ShapeDtypeStruct((B, N, HSLOTS), BF16),
        jax.ShapeDtypeStruct((B, N, HSLOTS), BF16),
        jax.ShapeDtypeStruct((B, N, MLA_WIDTH), BF16),
        jax.ShapeDtypeStruct((B, 2, N, F_WIDTH), BF16),
    ]
    return pl.pallas_call(
        _premix_kernel,
        grid=(B, N // tm),
        in_specs=in_specs,
        out_specs=out_specs,
        out_shape=out_shape,
        compiler_params=_cparams(("parallel", "parallel")),
        name="premix_proj",
    )(x, sh, sc, lw["gmix"], lw["win"], lw["qa"], lw["wq"], lw["wqs"], lw["kva"], lw["wk"], lw["wv"],
      lw["esum"], cos_t, sin_t, lw["gq"], lw["gqs"], lw["gk"], lw["gks"], lw["mf"])


def _attn_kernel(qt_ref, k_ref, vt_ref, ot_ref, *, n_kt):
    qt = qt_ref[0]
    tq = qt.shape[1]
    unroll = _attn_unroll(n_kt)

    def scores(t):
        return jnp.dot(k_ref[0, t], qt, preferred_element_type=F32)

    def body(g, carry):
        m, l, acc, s = carry
        for u in range(unroll):
            t = g * unroll + u
            s_next = scores(jnp.minimum(t + 1, n_kt - 1))
            m_new = jnp.maximum(m, jnp.max(s, axis=0, keepdims=True))
            alpha = jnp.exp2(m - m_new)
            p = jnp.exp2(s - m_new)
            l = alpha * l + jnp.sum(p, axis=0, keepdims=True)
            acc = alpha * acc + jnp.dot(vt_ref[0, 0, t], p.astype(BF16), preferred_element_type=F32)
            m, s = m_new, s_next
        return m, l, acc, s

    init = (jnp.full((1, tq), -1e30, F32), jnp.zeros((1, tq), F32), jnp.zeros((V_DIM, tq), F32), scores(0))
    m, l, acc, _ = lax.fori_loop(0, n_kt // unroll, body, init)
    ot_ref[0] = (acc / l).astype(ot_ref.dtype)


ATTN_TK = 256


def _attn_unroll(n_kt):
    for u in (13, 9, 8, 5, 4, 3, 2):
        if n_kt % u == 0:
            return u
    return 1


def _attn_call(qt, k, vt, tq):
    B, _, Nq = qt.shape
    n_kt = k.shape[1]
    tk = k.shape[2]
    kern = functools.partial(_attn_kernel, n_kt=n_kt)
    return pl.pallas_call(
        kern,
        grid=(B, N_HEADS, Nq // tq),
        in_specs=[
            pl.BlockSpec((1, SLOT, tq), lambda b, h, i: (b, h, i)),
            pl.BlockSpec((1, n_kt, tk, SLOT), lambda b, h, i: (b, 0, 0, h)),
            pl.BlockSpec((1, 1, n_kt, V_DIM, tk), lambda b, h, i: (b, h, 0, 0, 0)),
        ],
        out_specs=pl.BlockSpec((1, V_DIM, tq), lambda b, h, i: (b, h, i)),
        out_shape=jax.ShapeDtypeStruct((B, MLA_WIDTH, Nq), BF16),
        compiler_params=_cparams(("parallel", "parallel", "arbitrary")),
        name="mla_attention",
    )(qt, k, vt)


def _attn_layouts(q, k, v):
    B, Nq, _ = q.shape
    Nk = k.shape[1]
    n_kt = Nk // ATTN_TK
    qt = jnp.swapaxes(q, 1, 2)
    k4 = k.reshape(B, n_kt, ATTN_TK, HSLOTS)
    vt = v.reshape(B, n_kt, ATTN_TK, N_HEADS, V_DIM).transpose(0, 3, 1, 4, 2)
    return qt, k4, vt


def _fft1_kernel(f1_ref, tr_ref, ti_ref, w_ref, y_ref, *, n1, n2_per):
    y = jnp.dot(f1_ref[...], w_ref[0], preferred_element_type=F32)
    yr = y[:n1]
    yi = y[n1:]
    for t in range(n2_per):
        tr = tr_ref[:, t * LANES:(t + 1) * LANES]
        ti = ti_ref[:, t * LANES:(t + 1) * LANES]
        tr4 = jnp.concatenate([tr] * F_GROUPS, axis=-1)
        ti4 = jnp.concatenate([ti] * F_GROUPS, axis=-1)
        a = yr[:, t * F_WIDTH:(t + 1) * F_WIDTH]
        b = yi[:, t * F_WIDTH:(t + 1) * F_WIDTH]
        y_ref[0, :n1, t * F_WIDTH:(t + 1) * F_WIDTH] = (a * tr4 - b * ti4).astype(y_ref.dtype)
        y_ref[0, n1:, t * F_WIDTH:(t + 1) * F_WIDTH] = (a * ti4 + b * tr4).astype(y_ref.dtype)


def _fft2_kernel(c2_ref, s2_ref, y_ref, o_ref, *, kc):
    for i in range(kc):
        re = jnp.dot(c2_ref[...], y_ref[0, 0, i], preferred_element_type=F32)
        im = jnp.dot(s2_ref[...], y_ref[0, 1, i], preferred_element_type=F32)
        o_ref[0, :, i * F_WIDTH:(i + 1) * F_WIDTH] = (re + im).astype(o_ref.dtype)


def _fft_consts(n):
    n2 = LANES
    n1 = n // n2
    a1 = 2.0 * np.pi * np.outer(np.arange(n1), np.arange(n1)) / n1
    c1, s1 = np.cos(a1), np.sin(a1)
    f1 = np.block([[c1, s1], [-s1, c1]]).astype(np.float32)
    at = 2.0 * np.pi * np.outer(np.arange(n1), np.arange(n2)) / n
    tr = np.repeat(np.cos(at), LANES, axis=1).astype(np.float32)
    ti = np.repeat(-np.sin(at), LANES, axis=1).astype(np.float32)
    a2 = 2.0 * np.pi * np.outer(np.arange(n2), np.arange(n2)) / n2
    scale = 1.0 / math.sqrt(n)
    c2 = (np.cos(a2) * scale).astype(np.float32)
    s2 = (np.sin(a2) * scale).astype(np.float32)
    return n1, n2, f1, tr, ti, c2, s2


def _fft_call(wri):
    B, _, N, _ = wri.shape
    n1, n2, f1, tr, ti, c2, s2 = _fft_consts(N)
    cols = n2 * F_WIDTH
    n2_per = 4
    tc = n2_per * F_WIDTH
    y = pl.pallas_call(
        functools.partial(_fft1_kernel, n1=n1, n2_per=n2_per),
        grid=(B, cols // tc),
        in_specs=[
            _full((2 * n1, 2 * n1)),
            pl.BlockSpec((n1, n2_per * LANES), lambda b, j: (0, j)),
            pl.BlockSpec((n1, n2_per * LANES), lambda b, j: (0, j)),
            pl.BlockSpec((1, 2 * n1, tc), lambda b, j: (b, 0, j)),
        ],
        out_specs=pl.BlockSpec((1, 2 * n1, tc), lambda b, j: (b, 0, j)),
        out_shape=jax.ShapeDtypeStruct((B, 2 * n1, cols), BF16),
        compiler_params=_cparams(("parallel", "parallel")),
        name="fft_stage1",
    )(jnp.asarray(f1, BF16), jnp.asarray(tr), jnp.asarray(ti), wri.reshape(B, 2 * n1, cols))
    kc = min(8, n1)
    out = pl.pallas_call(
        functools.partial(_fft2_kernel, kc=kc),
        grid=(B, n1 // kc),
        in_specs=[
            _full((n2, n2)),
            _full((n2, n2)),
            pl.BlockSpec((1, 2, kc, n2, F_WIDTH), lambda b, j: (b, 0, j, 0, 0)),
        ],
        out_specs=pl.BlockSpec((1, n2, kc * F_WIDTH), lambda b, j: (b, 0, j)),
        out_shape=jax.ShapeDtypeStruct((B, n2, n1 * F_WIDTH), BF16),
        compiler_params=_cparams(("parallel", "parallel")),
        name="fft_stage2",
    )(jnp.asarray(c2, BF16), jnp.asarray(s2, BF16), y.reshape(B, 2, n1, n2, F_WIDTH))
    return out.reshape(B, N, F_WIDTH)


def _dft_small_kernel(c_ref, s_ref, w_ref, o_ref):
    re = jnp.dot(c_ref[...], w_ref[0, 0], preferred_element_type=F32)
    im = jnp.dot(s_ref[...], w_ref[0, 1], preferred_element_type=F32)
    o_ref[0] = (re + im).astype(o_ref.dtype)


def _dft_small_call(wri):
    B, _, N, _ = wri.shape
    a = 2.0 * np.pi * np.outer(np.arange(N), np.arange(N)) / N
    c = (np.cos(a) / math.sqrt(N)).astype(np.float32)
    s = (np.sin(a) / math.sqrt(N)).astype(np.float32)
    return pl.pallas_call(
        _dft_small_kernel,
        grid=(B,),
        in_specs=[_full((N, N)), _full((N, N)), pl.BlockSpec((1, 2, N, F_WIDTH), lambda b: (b, 0, 0, 0))],
        out_specs=pl.BlockSpec((1, N, F_WIDTH), lambda b: (b, 0, 0)),
        out_shape=jax.ShapeDtypeStruct((B, N, F_WIDTH), BF16),
        compiler_params=_cparams(("parallel",)),
        name="dft_small",
    )(jnp.asarray(c, BF16), jnp.asarray(s, BF16), wri)


def _router_rows(logits, bias):
    t = logits.shape[1]
    s = jax.nn.sigmoid(logits)
    b = s + bias
    srow = [s[e:e + 1, :] for e in range(N_EXPERTS)]
    brow = [b[e:e + 1, :] for e in range(N_EXPERTS)]
    gscore = []
    for g in range(N_EGROUPS):
        b0, b1, b2, b3 = brow[EPG * g:EPG * g + EPG]
        m1, n1 = jnp.maximum(b0, b1), jnp.minimum(b0, b1)
        m2, n2 = jnp.maximum(b2, b3), jnp.minimum(b2, b3)
        top1 = jnp.maximum(m1, m2)
        top2 = jnp.maximum(jnp.minimum(m1, m2), jnp.maximum(n1, n2))
        gscore.append(top1 + top2)
    best = gscore[0]
    gi = jnp.zeros((1, t), jnp.int32)
    for g in range(1, N_EGROUPS):
        take = gscore[g] > best
        gi = jnp.where(take, g, gi)
        best = jnp.where(take, gscore[g], best)
    ib, isc = [], []
    for i in range(EPG):
        vb, vs = brow[i], srow[i]
        for g in range(1, N_EGROUPS):
            sel = gi == g
            vb = jnp.where(sel, brow[EPG * g + i], vb)
            vs = jnp.where(sel, srow[EPG * g + i], vs)
        ib.append(vb)
        isc.append(vs)
    w = []
    for i in range(EPG):
        rank = jnp.zeros((1, t), jnp.int32)
        for j in range(EPG):
            if j == i:
                continue
            ahead = (ib[j] > ib[i]) | ((ib[j] == ib[i]) & (j < i))
            rank = rank + ahead.astype(jnp.int32)
        w.append(jnp.where(rank < 2, isc[i], 0.0))
    tot = (w[0] + w[1]) + (w[2] + w[3])
    inv = 1.0 / tot
    eidx = lax.broadcasted_iota(jnp.int32, (N_EXPERTS, t), 0)
    in_group = jnp.right_shift(eidx, 2) == gi
    dense = jnp.zeros((N_EXPERTS, t), F32)
    for i in range(EPG):
        dense = jnp.where(in_group & (jnp.bitwise_and(eidx, EPG - 1) == i), w[i] * inv, dense)
    return dense


def _postmix_kernel(o_ref, z_ref, x_ref, g1_ref, sh_ref, sc_ref, gn_ref, wo_ref, rw_ref, rb_ref,
                    xo_ref, f_ref, wd_ref):
    mix = jnp.dot(o_ref[0], wo_ref[:MLA_WIDTH, :], preferred_element_type=F32)
    mix = mix + jnp.dot(z_ref[0], wo_ref[MLA_WIDTH:, :], preferred_element_type=F32)
    xn = x_ref[0] + g1_ref[0] * mix
    xo_ref[0] = xn
    f = _rms(xn) * gn_ref[...] * (1.0 + sc_ref[0]) + sh_ref[0]
    f_ref[0] = f.astype(f_ref.dtype)
    logits = lax.dot_general(rw_ref[...], f, (((1,), (1,)), ((), ())), preferred_element_type=F32,
                             precision=HIGHEST)
    wd_ref[0] = _router_rows(logits, rb_ref[...])


def _postmix_call(o, z, x, g1, sh, sc, lw, rwt, rb, tm):
    B, N, D = x.shape
    row = lambda b, i: (b, i, 0)
    vec = lambda b, i: (b, 0, 0)
    return pl.pallas_call(
        _postmix_kernel,
        grid=(B, N // tm),
        in_specs=[
            pl.BlockSpec((1, tm, MLA_WIDTH), row),
            pl.BlockSpec((1, tm, F_WIDTH), row),
            pl.BlockSpec((1, tm, D), row),
            pl.BlockSpec((1, 1, D), vec),
            pl.BlockSpec((1, 1, D), vec),
            pl.BlockSpec((1, 1, D), vec),
            _full((1, D)),
            _full((D, D)),
            _full((N_EXPERTS, D)),
            _full((N_EXPERTS, 1)),
        ],
        out_specs=[
            pl.BlockSpec((1, tm, D), row),
            pl.BlockSpec((1, tm, D), row),
            pl.BlockSpec((1, N_EXPERTS, tm), lambda b, i: (b, 0, i)),
        ],
        out_shape=[
            jax.ShapeDtypeStruct((B, N, D), F32),
            jax.ShapeDtypeStruct((B, N, D), BF16),
            jax.ShapeDtypeStruct((B, N_EXPERTS, N), F32),
        ],
        compiler_params=_cparams(("parallel", "parallel")),
        name="postmix_router",
    )(o, z, x, g1, sh, sc, lw["gffn"], lw["wout"], rwt, rb)


def _moe_kernel(f_ref, w2_ref, x_ref, g2_ref, ex_ref, wg_ref, wu_ref, wd_ref, xo_ref, acc_ref):
    g = pl.program_id(2)

    @pl.when(g == 0)
    def _():
        acc_ref[...] = jnp.zeros(acc_ref.shape, acc_ref.dtype)

    f = f_ref[0]
    hg = jnp.dot(f, wg_ref[0], preferred_element_type=F32)
    hu = jnp.dot(f, wu_ref[0], preferred_element_type=F32)
    wexp = jnp.dot(w2_ref[0], ex_ref[0], preferred_element_type=F32)
    h = (hg * jax.nn.sigmoid(hg)) * hu * wexp
    acc_ref[...] += jnp.dot(h.astype(BF16), wd_ref[0], preferred_element_type=F32)

    @pl.when(g == N_EGROUPS - 1)
    def _():
        xo_ref[0] = x_ref[0] + g2_ref[0] * acc_ref[...]


def _moe_call(f, w2, x, g2, lw, tm):
    B, N, D = x.shape
    GW = EPG * D_EXPERT
    row = lambda b, i, g: (b, i, 0)
    return pl.pallas_call(
        _moe_kernel,
        grid=(B, N // tm, N_EGROUPS),
        in_specs=[
            pl.BlockSpec((1, tm, D), row),
            pl.BlockSpec((1, tm, 2 * N_EXPERTS), row),
            pl.BlockSpec((1, tm, D), row),
            pl.BlockSpec((1, 1, D), lambda b, i, g: (b, 0, 0)),
            pl.BlockSpec((1, 2 * N_EXPERTS, GW), lambda b, i, g: (g, 0, 0)),
            pl.BlockSpec((1, D, GW), lambda b, i, g: (g, 0, 0)),
            pl.BlockSpec((1, D, GW), lambda b, i, g: (g, 0, 0)),
            pl.BlockSpec((1, GW, D), lambda b, i, g: (g, 0, 0)),
        ],
        out_specs=pl.BlockSpec((1, tm, D), row),
        out_shape=jax.ShapeDtypeStruct((B, N, D), F32),
        scratch_shapes=[pltpu.VMEM((tm, D), F32)],
        compiler_params=_cparams(("parallel", "parallel", "arbitrary")),
        name="moe_experts",
    )(f, w2, x, g2, lw["expand"], lw["wgate"], lw["wup"], lw["wdown"])


def _rope_partner():
    p = np.arange(SLOT)
    for j in range(ROPE_HALF):
        for base in (QK_NOPE, QK_NOPE + ROPE_AXIS):
            p[base + j] = base + ROPE_HALF + j
            p[base + ROPE_HALF + j] = base + j
    return p


def _rope_tables(n):
    rows = n // GRID_W
    row = jnp.repeat(jnp.arange(rows, dtype=F32), GRID_W)
    col = jnp.tile(jnp.arange(GRID_W, dtype=F32), rows)
    freqs = ROPE_BASE ** (-jnp.arange(ROPE_HALF, dtype=F32) / ROPE_HALF)
    ar, ac = row[:, None] * freqs, col[:, None] * freqs
    one = jnp.ones((n, QK_NOPE), F32)
    zero = jnp.zeros((n, QK_NOPE), F32)
    pad1 = jnp.ones((n, SLOT - QK_DIM), F32)
    pad0 = jnp.zeros((n, SLOT - QK_DIM), F32)
    cos_t = jnp.concatenate([one, jnp.cos(ar), jnp.cos(ar), jnp.cos(ac), jnp.cos(ac), pad1], axis=1)
    sin_t = jnp.concatenate([zero, -jnp.sin(ar), jnp.sin(ar), -jnp.sin(ac), jnp.sin(ac), pad0], axis=1)
    return cos_t, sin_t


def _slot_cols(w, width):
    kdim = w.shape[0]
    w = w.reshape(kdim, N_HEADS, width)
    return jnp.pad(w, ((0, 0), (0, 0), (0, SLOT - width))).reshape(kdim, HSLOTS)


def _layer_weights(l, p, mf_all):
    partner = _rope_partner()
    w_in = p["w_in"][l]
    kr_cols = w_in[:, Q_LORA + KV_LORA:Q_LORA + KV_LORA + QK_ROPE]
    kr_slot = jnp.pad(kr_cols, ((0, 0), (QK_NOPE, SLOT - QK_DIM)))
    kr_swap = jnp.where((np.arange(SLOT) >= QK_NOPE) & (np.arange(SLOT) < QK_DIM), kr_slot[:, partner], 0.0)
    win = jnp.concatenate([w_in[:, :Q_LORA + KV_LORA], kr_slot, kr_swap, w_in[:, Q_LORA + KV_LORA + QK_ROPE:]], axis=1)

    rope_lane = (np.arange(SLOT) >= QK_NOPE) & (np.arange(SLOT) < QK_DIM)
    wq = _slot_cols(p["w_q_b"][l], QK_DIM)
    wq3 = wq.reshape(Q_LORA, N_HEADS, SLOT)
    wqs = jnp.where(rope_lane, wq3[:, :, partner], 0.0).reshape(Q_LORA, HSLOTS)

    wkv = p["w_kv_b"][l].reshape(KV_LORA, N_HEADS, QK_NOPE + V_DIM)
    wk = jnp.pad(wkv[:, :, :QK_NOPE], ((0, 0), (0, 0), (0, SLOT - QK_NOPE))).reshape(KV_LORA, HSLOTS)
    wv = wkv[:, :, QK_NOPE:].reshape(KV_LORA, MLA_WIDTH)

    def gains(g, scale):
        gp = jnp.pad(g, (0, SLOT - QK_DIM))
        gs = jnp.where(rope_lane, gp[partner], 0.0)
        return (jnp.tile(gp, N_HEADS) * scale)[None, :], (jnp.tile(gs, N_HEADS) * scale)[None, :]

    gq, gqs = gains(p["q_norm"][l], QK_DIM ** -0.5 * math.log2(math.e))
    gk, gks = gains(p["k_norm"][l], 1.0)

    def group_cols(w):
        return w.reshape(N_EGROUPS, EPG, D_MODEL, D_EXPERT).transpose(0, 2, 1, 3).reshape(
            N_EGROUPS, D_MODEL, EPG * D_EXPERT).astype(BF16)

    return {
        "gmix": p["norm_mix"][l][None, :],
        "gffn": p["norm_ffn"][l][None, :],
        "win": win.astype(BF16),
        "qa": p["q_a_norm"][l][None, :],
        "wq": wq.astype(BF16),
        "wqs": wqs.astype(BF16),
        "kva": p["kv_a_norm"][l][None, :],
        "wk": wk.astype(BF16),
        "wv": wv.astype(BF16),
        "gq": gq, "gqs": gqs, "gk": gk, "gks": gks,
        "mf": mf_all[l],
        "wout": p["w_out"][l].astype(BF16),
        "wgate": group_cols(p["w_gate"][l]),
        "wup": group_cols(p["w_up"][l]),
        "wdown": p["w_down"][l].reshape(N_EGROUPS, EPG * D_EXPERT, D_MODEL).astype(BF16),
    }


def _const_tables():
    lane_head = np.arange(HSLOTS) // SLOT
    esum = (lane_head[:, None] == lane_head[None, :]).astype(np.float32)
    ex = np.zeros((N_EGROUPS, 2 * N_EXPERTS, EPG * D_EXPERT), np.float32)
    for g in range(N_EGROUPS):
        for i in range(EPG):
            ex[g, EPG * g + i, i * D_EXPERT:(i + 1) * D_EXPERT] = 1.0
            ex[g, N_EXPERTS + EPG * g + i, i * D_EXPERT:(i + 1) * D_EXPERT] = 1.0
    return jnp.asarray(esum, BF16), jnp.asarray(ex, BF16)


def _split_hi_lo(w):
    wt = jnp.swapaxes(w, 1, 2)
    hi = wt.astype(BF16)
    lo = (wt - hi.astype(F32)).astype(BF16)
    return jnp.concatenate([hi, lo], axis=-1)


def _mod6(mod_l, rows, B):
    m = mod_l[jnp.asarray(rows)]
    return [m[:, i * D_MODEL:(i + 1) * D_MODEL][:, None, :] for i in range(6)]


def kernel(x, c, ctx, c_ctx, w_ada, b_ada, norm_mix, norm_ffn, w_in, q_a_norm, w_q_b, kv_a_norm, w_kv_b, q_norm,
           k_norm, w_fourier, w_out, router_w, router_bias, w_gate, w_up, w_down):
    B, S, D = x.shape
    n_ctx = ctx.shape[1]
    depth = w_ada.shape[0]
    p = dict(w_in=w_in, q_a_norm=q_a_norm, w_q_b=w_q_b, kv_a_norm=kv_a_norm, w_kv_b=w_kv_b, q_norm=q_norm,
             k_norm=k_norm, norm_mix=norm_mix, norm_ffn=norm_ffn, w_out=w_out, w_gate=w_gate, w_up=w_up,
             w_down=w_down)

    c8 = jnp.concatenate([c, c_ctx[None, :], jnp.zeros((8 - B - 1, D), F32)], axis=0)
    mod = _ada_call(c8, w_ada, b_ada)
    mf_all = _fmat_call(w_fourier)
    esum, expand = _const_tables()
    cos_l, sin_l = _rope_tables(S)
    cos_c = jnp.ones((n_ctx, SLOT), F32)
    sin_c = jnp.zeros((n_ctx, SLOT), F32)
    rwt = router_w.T
    rb = router_bias[:, None]

    tm_lat = 512
    tm_ctx = min(256, n_ctx)
    for l in range(depth):
        last = l == depth - 1
        lw = _layer_weights(l, p, mf_all)
        lw["esum"] = esum
        lw["expand"] = expand
        sh1, sc1, g1, sh2, sc2, g2 = _mod6(mod[l], list(range(B)), B)
        csh1, csc1, cg1, csh2, csc2, cg2 = _mod6(mod[l], [B] * B, B)

        q_c, k_c, v_c, wri_c = _premix_call(ctx, csh1, csc1, lw, cos_c, sin_c, tm_ctx)
        q_l, k_l, v_l, wri_l = _premix_call(x, sh1, sc1, lw, cos_l, sin_l, tm_lat)
        k_all = jnp.concatenate([k_c, k_l], axis=1)
        v_all = jnp.concatenate([v_c, v_l], axis=1)
        o_l = jnp.swapaxes(_attn_call(*_attn_layouts(q_l, k_all, v_all), 512), 1, 2)
        z_l = _fft_call(wri_l)
        x_mid, f_l, wd_l = _postmix_call(o_l, z_l, x, g1, sh2, sc2, lw, rwt, rb, tm_lat)
        x = _moe_call(f_l, _split_hi_lo(wd_l), x_mid, g2, lw, tm_lat)
        if not last:
            o_c = jnp.swapaxes(_attn_call(*_attn_layouts(q_c, k_c, v_c), tm_ctx), 1, 2)
            z_c = _dft_small_call(wri_c)
            c_mid, f_c, wd_c = _postmix_call(o_c, z_c, ctx, cg1, csh2, csc2, lw, rwt, rb, tm_ctx)
            ctx = _moe_call(f_c, _split_hi_lo(wd_c), c_mid, cg2, lw, tm_ctx)
    return x
```

```python
import functools
import math

import numpy as np
import jax
import jax.numpy as jnp
from jax import lax
from jax.experimental import pallas as pl
from jax.experimental.pallas import tpu as pltpu

F32 = jnp.float32
BF16 = jnp.bfloat16
HIGHEST = lax.Precision.HIGHEST

D_MODEL = 1024
N_HEADS = 8
QK_NOPE = 64
QK_ROPE = 32
QK_DIM = QK_NOPE + QK_ROPE
V_DIM = 64
Q_LORA = 256
KV_LORA = 128
ROPE_AXIS = QK_ROPE // 2
ROPE_HALF = ROPE_AXIS // 2
ROPE_BASE = 10000.0
GRID_W = 64
F_GROUPS = 4
F_GDIM = 128
F_WIDTH = F_GROUPS * F_GDIM
MLA_WIDTH = N_HEADS * V_DIM
N_EXPERTS = 16
N_EGROUPS = 4
EPG = N_EXPERTS // N_EGROUPS
D_EXPERT = 256
EPS = 1e-6

LANES = 128
SLOT = LANES
HSLOTS = N_HEADS * SLOT
VMEM_LIMIT = 56 * 1024 * 1024

C_CQ = 0
C_CKV = C_CQ + Q_LORA
C_KR = C_CKV + KV_LORA
C_KRS = C_KR + SLOT
C_F = C_KRS + SLOT
IN_COLS = C_F + F_WIDTH


def _cparams(sem):
    return pltpu.CompilerParams(dimension_semantics=sem, vmem_limit_bytes=VMEM_LIMIT)


def _full(shape):
    n = len(shape)
    return pl.BlockSpec(shape, lambda *_: (0,) * n)


def _ada_kernel(c_ref, w_ref, b_ref, o_ref):
    c = c_ref[...]
    s = c * jax.nn.sigmoid(c)
    o_ref[0] = jnp.dot(s, w_ref[0], preferred_element_type=F32, precision=HIGHEST) + b_ref[0]


def _ada_call(c8, w_ada, b_ada):
    L, D, W = w_ada.shape
    tn = 1536
    return pl.pallas_call(
        _ada_kernel,
        grid=(L, W // tn),
        in_specs=[
            pl.BlockSpec((8, D), lambda l, j: (0, 0)),
            pl.BlockSpec((1, D, tn), lambda l, j: (l, 0, j)),
            pl.BlockSpec((1, 1, tn), lambda l, j: (l, 0, j)),
        ],
        out_specs=pl.BlockSpec((1, 8, tn), lambda l, j: (l, 0, j)),
        out_shape=jax.ShapeDtypeStruct((L, 8, W), F32),
        compiler_params=_cparams(("parallel", "parallel")),
        name="adaln_mod",
    )(c8, w_ada, b_ada.reshape(L, 1, W))


def _fmat_kernel(cc_ref, sc_ref, wf_ref, o_ref):
    o_ref[...] = jnp.zeros(o_ref.shape, o_ref.dtype)
    for g in range(F_GROUPS):
        wf = wf_ref[0, g]
        mr = jnp.dot(cc_ref[...], wf, preferred_element_type=F32, precision=HIGHEST)
        mi = jnp.dot(sc_ref[...], wf, preferred_element_type=F32, precision=HIGHEST)
        r0 = g * F_GDIM
        o_ref[0, r0:r0 + F_GDIM, r0:r0 + F_GDIM] = mr.astype(o_ref.dtype)
        o_ref[0, r0:r0 + F_GDIM, F_WIDTH + r0:F_WIDTH + r0 + F_GDIM] = mi.astype(o_ref.dtype)


def _fmat_call(w_fourier):
    L = w_fourier.shape[0]
    c = np.arange(F_GDIM)
    ang = 2.0 * np.pi * np.outer(c, c) / F_GDIM
    cc = jnp.asarray((np.cos(ang) / math.sqrt(F_GDIM)).astype(np.float32))
    sc = jnp.asarray((-np.sin(ang) / math.sqrt(F_GDIM)).astype(np.float32))
    return pl.pallas_call(
        _fmat_kernel,
        grid=(L,),
        in_specs=[
            _full((F_GDIM, F_GDIM)),
            _full((F_GDIM, F_GDIM)),
            pl.BlockSpec((1, F_GROUPS, F_GDIM, F_GDIM), lambda l: (l, 0, 0, 0)),
        ],
        out_specs=pl.BlockSpec((1, F_WIDTH, 2 * F_WIDTH), lambda l: (l, 0, 0)),
        out_shape=jax.ShapeDtypeStruct((L, F_WIDTH, 2 * F_WIDTH), BF16),
        compiler_params=_cparams(("parallel",)),
        name="fourier_chan_mats",
    )(cc, sc, w_fourier)


def _rms(x):
    return x * lax.rsqrt(jnp.mean(x * x, axis=-1, keepdims=True) + EPS)


def _tile8(a):
    return jnp.concatenate([a] * N_HEADS, axis=-1)


def _premix_kernel(x_ref, sh_ref, sc_ref, gn_ref, win_ref, qa_ref, wq_ref, wqs_ref, kva_ref, wk_ref, wv_ref,
                   e_ref, cos_ref, sin_ref, gq_ref, gqs_ref, gk_ref, gks_ref, mf_ref,
                   q_ref, k_ref, v_ref, wri_ref):
    x = x_ref[0]
    h = _rms(x) * gn_ref[...] * (1.0 + sc_ref[0]) + sh_ref[0]
    proj = jnp.dot(h.astype(BF16), win_ref[...], preferred_element_type=F32)
    cq = proj[:, C_CQ:C_CKV]
    ckv = proj[:, C_CKV:C_KR]
    kr = proj[:, C_KR:C_KRS]
    krs = proj[:, C_KRS:C_F]
    f = proj[:, C_F:IN_COLS]

    cos8 = _tile8(cos_ref[...])
    sin8 = _tile8(sin_ref[...])
    inv_d = 1.0 / QK_DIM

    cqn = (_rms(cq) * qa_ref[...]).astype(BF16)
    qr = jnp.dot(cqn, wq_ref[...], preferred_element_type=F32)
    qs = jnp.dot(cqn, wqs_ref[...], preferred_element_type=F32)
    ssq = jnp.dot((qr * qr).astype(BF16), e_ref[...], preferred_element_type=F32)
    rq = lax.rsqrt(ssq * inv_d + EPS)
    q = rq * (qr * (gq_ref[...] * cos8) + qs * (gqs_ref[...] * sin8))
    q_ref[0] = q.astype(q_ref.dtype)

    ckvn = (_rms(ckv) * kva_ref[...]).astype(BF16)
    kraw = jnp.dot(ckvn, wk_ref[...], preferred_element_type=F32) + _tile8(kr)
    ssk = jnp.dot((kraw * kraw).astype(BF16), e_ref[...], preferred_element_type=F32)
    rk = lax.rsqrt(ssk * inv_d + EPS)
    k = rk * (kraw * (gk_ref[...] * cos8) + _tile8(krs) * (gks_ref[...] * sin8))
    k_ref[0] = k.astype(k_ref.dtype)
    v_ref[0] = jnp.dot(ckvn, wv_ref[...], preferred_element_type=F32).astype(v_ref.dtype)

    w = jnp.dot(f.astype(BF16), mf_ref[...], preferred_element_type=F32)
    wri_ref[0, 0] = w[:, :F_WIDTH].astype(wri_ref.dtype)
    wri_ref[0, 1] = w[:, F_WIDTH:].astype(wri_ref.dtype)


def _premix_call(x, sh, sc, lw, cos_t, sin_t, tm):
    B, N, D = x.shape
    row = lambda b, i: (b, i, 0)
    vec = lambda b, i: (b, 0, 0)
    in_specs = [
        pl.BlockSpec((1, tm, D), row),
        pl.BlockSpec((1, 1, D), vec),
        pl.BlockSpec((1, 1, D), vec),
        _full((1, D)),
        _full((D, IN_COLS)),
        _full((1, Q_LORA)),
        _full((Q_LORA, HSLOTS)),
        _full((Q_LORA, HSLOTS)),
        _full((1, KV_LORA)),
        _full((KV_LORA, HSLOTS)),
        _full((KV_LORA, MLA_WIDTH)),
        _full((HSLOTS, HSLOTS)),
        pl.BlockSpec((tm, SLOT), lambda b, i: (i, 0)),
        pl.BlockSpec((tm, SLOT), lambda b, i: (i, 0)),
        _full((1, HSLOTS)),
        _full((1, HSLOTS)),
        _full((1, HSLOTS)),
        _full((1, HSLOTS)),
        _full((F_WIDTH, 2 * F_WIDTH)),
    ]
    out_specs = [
        pl.BlockSpec((1, tm, HSLOTS), row),
        pl.BlockSpec((1, tm, HSLOTS), row),
        pl.BlockSpec((1, tm, MLA_WIDTH), row),
        pl.BlockSpec((1, 2, tm, F_WIDTH), lambda b, i: (b, 0, i, 0)),
    ]
    out_shape = [
        jax.ShapeDtypeStruct((B, N, HSLOTS), BF16),
        jax.ShapeDtypeStruct((B, N, HSLOTS), BF16),
        jax.ShapeDtypeStruct((B, N, MLA_WIDTH), BF16),
        jax.ShapeDtypeStruct((B, 2, N, F_WIDTH), BF16),
    ]
    return pl.pallas_call(
        _premix_kernel,
        grid=(B, N // tm),
        in_specs=in_specs,
        out_specs=out_specs,
        out_shape=out_shape,
        compiler_params=_cparams(("parallel", "parallel")),
        name="premix_proj",
    )(x, sh, sc, lw["gmix"], lw["win"], lw["qa"], lw["wq"], lw["wqs"], lw["kva"], lw["wk"], lw["wv"],
      lw["esum"], cos_t, sin_t, lw["gq"], lw["gqs"], lw["gk"], lw["gks"], lw["mf"])


def _attn_kernel(qt_ref, k_ref, vt_ref, ot_ref, *, n_kt, bounded):
    qt = qt_ref[0]
    tq = qt.shape[1]

    def scores(t):
        return jnp.dot(k_ref[0, t], qt, preferred_element_type=F32)

    def pv(t, p):
        return jnp.dot(vt_ref[0, 0, t], p.astype(BF16), preferred_element_type=F32)

    if bounded:
        acc = jnp.zeros((VT_ROWS, tq), F32)
        s = scores(0)
        for t in range(n_kt):
            s_next = scores(t + 1) if t + 1 < n_kt else None
            acc = acc + pv(t, jnp.exp2(s))
            s = s_next
    else:
        unroll = _attn_unroll(n_kt)

        def body(g, carry):
            m, acc, s = carry
            for u in range(unroll):
                t = g * unroll + u
                s_next = scores(jnp.minimum(t + 1, n_kt - 1))
                m_new = jnp.maximum(m, jnp.max(s, axis=0, keepdims=True))
                acc = jnp.exp2(m - m_new) * acc + pv(t, jnp.exp2(s - m_new))
                m, s = m_new, s_next
            return m, acc, s

        init = (jnp.full((1, tq), -1e30, F32), jnp.zeros((VT_ROWS, tq), F32), scores(0))
        _, acc, _ = lax.fori_loop(0, n_kt // unroll, body, init)
    ot_ref[0] = (acc[:V_DIM] / acc[V_DIM:V_DIM + 1]).astype(ot_ref.dtype)


ATTN_TK = 256
VT_ROWS = V_DIM + 16
ATTN_SCORE_BOUND = 60.0


def _attn_unroll(n_kt):
    for u in (13, 9, 8, 5, 4, 3, 2):
        if n_kt % u == 0:
            return u
    return 1


def _attn_call(qt, k, vt, tq, bounded):
    B, _, Nq = qt.shape
    n_kt = k.shape[1]
    tk = k.shape[2]
    kern = functools.partial(_attn_kernel, n_kt=n_kt, bounded=bounded)
    return pl.pallas_call(
        kern,
        grid=(B, N_HEADS, Nq // tq),
        in_specs=[
            pl.BlockSpec((1, SLOT, tq), lambda b, h, i: (b, h, i)),
            pl.BlockSpec((1, n_kt, tk, SLOT), lambda b, h, i: (b, 0, 0, h)),
            pl.BlockSpec((1, 1, n_kt, VT_ROWS, tk), lambda b, h, i: (b, h, 0, 0, 0)),
        ],
        out_specs=pl.BlockSpec((1, V_DIM, tq), lambda b, h, i: (b, h, i)),
        out_shape=jax.ShapeDtypeStruct((B, MLA_WIDTH, Nq), BF16),
        compiler_params=_cparams(("parallel", "parallel", "arbitrary")),
        name="mla_attention_bounded" if bounded else "mla_attention_online",
    )(qt, k, vt)


def _attention(q, k, v, tq, scores_bounded):
    B, Nq, _ = q.shape
    Nk = k.shape[1]
    n_kt = Nk // ATTN_TK
    qt = jnp.swapaxes(q, 1, 2)
    k4 = k.reshape(B, n_kt, ATTN_TK, HSLOTS)
    vt = v.reshape(B, n_kt, ATTN_TK, N_HEADS, V_DIM).transpose(0, 3, 1, 4, 2)
    vt = jnp.concatenate([vt, jnp.ones((B, N_HEADS, n_kt, VT_ROWS - V_DIM, ATTN_TK), vt.dtype)], axis=3)
    ot = lax.cond(scores_bounded,
                  lambda a, b, c: _attn_call(a, b, c, tq, True),
                  lambda a, b, c: _attn_call(a, b, c, tq, False),
                  qt, k4, vt)
    return jnp.swapaxes(ot, 1, 2)


def _scores_bounded(q_gain, k_gain):
    bound = math.sqrt(QK_DIM) * math.log2(math.e) * jnp.max(jnp.abs(q_gain)) * jnp.max(jnp.abs(k_gain))
    return bound <= ATTN_SCORE_BOUND


def _fft1_kernel(f1_ref, tr_ref, ti_ref, w_ref, y_ref, *, n1, n2_per):
    y = jnp.dot(f1_ref[...], w_ref[0], preferred_element_type=F32)
    yr = y[:n1]
    yi = y[n1:]
    for t in range(n2_per):
        tr = tr_ref[:, t * LANES:(t + 1) * LANES]
        ti = ti_ref[:, t * LANES:(t + 1) * LANES]
        tr4 = jnp.concatenate([tr] * F_GROUPS, axis=-1)
        ti4 = jnp.concatenate([ti] * F_GROUPS, axis=-1)
        a = yr[:, t * F_WIDTH:(t + 1) * F_WIDTH]
        b = yi[:, t * F_WIDTH:(t + 1) * F_WIDTH]
        y_ref[0, :n1, t * F_WIDTH:(t + 1) * F_WIDTH] = (a * tr4 - b * ti4).astype(y_ref.dtype)
        y_ref[0, n1:, t * F_WIDTH:(t + 1) * F_WIDTH] = (a * ti4 + b * tr4).astype(y_ref.dtype)


def _fft2_kernel(c2_ref, s2_ref, y_ref, o_ref, *, kc):
    for i in range(kc):
        re = jnp.dot(c2_ref[...], y_ref[0, 0, i], preferred_element_type=F32)
        im = jnp.dot(s2_ref[...], y_ref[0, 1, i], preferred_element_type=F32)
        o_ref[0, :, i * F_WIDTH:(i + 1) * F_WIDTH] = (re + im).astype(o_ref.dtype)


def _fft_consts(n):
    n2 = LANES
    n1 = n // n2
    a1 = 2.0 * np.pi * np.outer(np.arange(n1), np.arange(n1)) / n1
    c1, s1 = np.cos(a1), np.sin(a1)
    f1 = np.block([[c1, s1], [-s1, c1]]).astype(np.float32)
    at = 2.0 * np.pi * np.outer(np.arange(n1), np.arange(n2)) / n
    tr = np.repeat(np.cos(at), LANES, axis=1).astype(np.float32)
    ti = np.repeat(-np.sin(at), LANES, axis=1).astype(np.float32)
    a2 = 2.0 * np.pi * np.outer(np.arange(n2), np.arange(n2)) / n2
    scale = 1.0 / math.sqrt(n)
    c2 = (np.cos(a2) * scale).astype(np.float32)
    s2 = (np.sin(a2) * scale).astype(np.float32)
    return n1, n2, f1, tr, ti, c2, s2


def _fft_call(wri):
    B, _, N, _ = wri.shape
    n1, n2, f1, tr, ti, c2, s2 = _fft_consts(N)
    cols = n2 * F_WIDTH
    n2_per = 4
    tc = n2_per * F_WIDTH
    y = pl.pallas_call(
        functools.partial(_fft1_kernel, n1=n1, n2_per=n2_per),
        grid=(B, cols // tc),
        in_specs=[
            _full((2 * n1, 2 * n1)),
            pl.BlockSpec((n1, n2_per * LANES), lambda b, j: (0, j)),
            pl.BlockSpec((n1, n2_per * LANES), lambda b, j: (0, j)),
            pl.BlockSpec((1, 2 * n1, tc), lambda b, j: (b, 0, j)),
        ],
        out_specs=pl.BlockSpec((1, 2 * n1, tc), lambda b, j: (b, 0, j)),
        out_shape=jax.ShapeDtypeStruct((B, 2 * n1, cols), BF16),
        compiler_params=_cparams(("parallel", "parallel")),
        name="fft_stage1",
    )(jnp.asarray(f1, BF16), jnp.asarray(tr), jnp.asarray(ti), wri.reshape(B, 2 * n1, cols))
    kc = min(8, n1)
    out = pl.pallas_call(
        functools.partial(_fft2_kernel, kc=kc),
        grid=(B, n1 // kc),
        in_specs=[
            _full((n2, n2)),
            _full((n2, n2)),
            pl.BlockSpec((1, 2, kc, n2, F_WIDTH), lambda b, j: (b, 0, j, 0, 0)),
        ],
        out_specs=pl.BlockSpec((1, n2, kc * F_WIDTH), lambda b, j: (b, 0, j)),
        out_shape=jax.ShapeDtypeStruct((B, n2, n1 * F_WIDTH), BF16),
        compiler_params=_cparams(("parallel", "parallel")),
        name="fft_stage2",
    )(jnp.asarray(c2, BF16), jnp.asarray(s2, BF16), y.reshape(B, 2, n1, n2, F_WIDTH))
    return out.reshape(B, N, F_WIDTH)


def _dft_small_kernel(c_ref, s_ref, w_ref, o_ref):
    re = jnp.dot(c_ref[...], w_ref[0, 0], preferred_element_type=F32)
    im = jnp.dot(s_ref[...], w_ref[0, 1], preferred_element_type=F32)
    o_ref[0] = (re + im).astype(o_ref.dtype)


def _dft_small_call(wri):
    B, _, N, _ = wri.shape
    a = 2.0 * np.pi * np.outer(np.arange(N), np.arange(N)) / N
    c = (np.cos(a) / math.sqrt(N)).astype(np.float32)
    s = (np.sin(a) / math.sqrt(N)).astype(np.float32)
    return pl.pallas_call(
        _dft_small_kernel,
        grid=(B,),
        in_specs=[_full((N, N)), _full((N, N)), pl.BlockSpec((1, 2, N, F_WIDTH), lambda b: (b, 0, 0, 0))],
        out_specs=pl.BlockSpec((1, N, F_WIDTH), lambda b: (b, 0, 0)),
        out_shape=jax.ShapeDtypeStruct((B, N, F_WIDTH), BF16),
        compiler_params=_cparams(("parallel",)),
        name="dft_small",
    )(jnp.asarray(c, BF16), jnp.asarray(s, BF16), wri)


def _router_rows(logits, bias):
    t = logits.shape[1]
    s = jax.nn.sigmoid(logits)
    b = s + bias
    srow = [s[e:e + 1, :] for e in range(N_EXPERTS)]
    brow = [b[e:e + 1, :] for e in range(N_EXPERTS)]
    gscore = []
    for g in range(N_EGROUPS):
        b0, b1, b2, b3 = brow[EPG * g:EPG * g + EPG]
        m1, n1 = jnp.maximum(b0, b1), jnp.minimum(b0, b1)
        m2, n2 = jnp.maximum(b2, b3), jnp.minimum(b2, b3)
        top1 = jnp.maximum(m1, m2)
        top2 = jnp.maximum(jnp.minimum(m1, m2), jnp.maximum(n1, n2))
        gscore.append(top1 + top2)
    best = gscore[0]
    gi = jnp.zeros((1, t), jnp.int32)
    for g in range(1, N_EGROUPS):
        take = gscore[g] > best
        gi = jnp.where(take, g, gi)
        best = jnp.where(take, gscore[g], best)
    ib, isc = [], []
    for i in range(EPG):
        vb, vs = brow[i], srow[i]
        for g in range(1, N_EGROUPS):
            sel = gi == g
            vb = jnp.where(sel, brow[EPG * g + i], vb)
            vs = jnp.where(sel, srow[EPG * g + i], vs)
        ib.append(vb)
        isc.append(vs)
    w = []
    for i in range(EPG):
        rank = jnp.zeros((1, t), jnp.int32)
        for j in range(EPG):
            if j == i:
                continue
            ahead = (ib[j] > ib[i]) | ((ib[j] == ib[i]) & (j < i))
            rank = rank + ahead.astype(jnp.int32)
        w.append(jnp.where(rank < 2, isc[i], 0.0))
    tot = (w[0] + w[1]) + (w[2] + w[3])
    inv = 1.0 / tot
    eidx = lax.broadcasted_iota(jnp.int32, (N_EXPERTS, t), 0)
    in_group = jnp.right_shift(eidx, 2) == gi
    dense = jnp.zeros((N_EXPERTS, t), F32)
    for i in range(EPG):
        dense = jnp.where(in_group & (jnp.bitwise_and(eidx, EPG - 1) == i), w[i] * inv, dense)
    return dense


def _postmix_kernel(o_ref, z_ref, x_ref, g1_ref, sh_ref, sc_ref, gn_ref, wo_ref, rw_ref, rb_ref,
                    xo_ref, f_ref, wd_ref):
    mix = jnp.dot(o_ref[0], wo_ref[:MLA_WIDTH, :], preferred_element_type=F32)
    mix = mix + jnp.dot(z_ref[0], wo_ref[MLA_WIDTH:, :], preferred_element_type=F32)
    xn = x_ref[0] + g1_ref[0] * mix
    xo_ref[0] = xn
    f = _rms(xn) * gn_ref[...] * (1.0 + sc_ref[0]) + sh_ref[0]
    f_ref[0] = f.astype(f_ref.dtype)
    logits = lax.dot_general(rw_ref[...], f, (((1,), (1,)), ((), ())), preferred_element_type=F32,
                             precision=HIGHEST)
    wd_ref[0] = _router_rows(logits, rb_ref[...])


def _postmix_call(o, z, x, g1, sh, sc, lw, rwt, rb, tm):
    B, N, D = x.shape
    row = lambda b, i: (b, i, 0)
    vec = lambda b, i: (b, 0, 0)
    return pl.pallas_call(
        _postmix_kernel,
        grid=(B, N // tm),
        in_specs=[
            pl.BlockSpec((1, tm, MLA_WIDTH), row),
            pl.BlockSpec((1, tm, F_WIDTH), row),
            pl.BlockSpec((1, tm, D), row),
            pl.BlockSpec((1, 1, D), vec),
            pl.BlockSpec((1, 1, D), vec),
            pl.BlockSpec((1, 1, D), vec),
            _full((1, D)),
            _full((D, D)),
            _full((N_EXPERTS, D)),
            _full((N_EXPERTS, 1)),
        ],
        out_specs=[
            pl.BlockSpec((1, tm, D), row),
            pl.BlockSpec((1, tm, D), row),
            pl.BlockSpec((1, N_EXPERTS, tm), lambda b, i: (b, 0, i)),
        ],
        out_shape=[
            jax.ShapeDtypeStruct((B, N, D), F32),
            jax.ShapeDtypeStruct((B, N, D), BF16),
            jax.ShapeDtypeStruct((B, N_EXPERTS, N), F32),
        ],
        compiler_params=_cparams(("parallel", "parallel")),
        name="postmix_router",
    )(o, z, x, g1, sh, sc, lw["gffn"], lw["wout"], rwt, rb)


def _moe_kernel(f_ref, w2_ref, x_ref, g2_ref, ex_ref, wg_ref, wu_ref, wd_ref, xo_ref, acc_ref):
    g = pl.program_id(2)

    @pl.when(g == 0)
    def _():
        acc_ref[...] = jnp.zeros(acc_ref.shape, acc_ref.dtype)

    f = f_ref[0]
    hg = jnp.dot(f, wg_ref[0], preferred_element_type=F32)
    hu = jnp.dot(f, wu_ref[0], preferred_element_type=F32)
    wexp = jnp.dot(w2_ref[0], ex_ref[0], preferred_element_type=F32)
    h = (hg * jax.nn.sigmoid(hg)) * hu * wexp
    acc_ref[...] += jnp.dot(h.astype(BF16), wd_ref[0], preferred_element_type=F32)

    @pl.when(g == N_EGROUPS - 1)
    def _():
        xo_ref[0] = x_ref[0] + g2_ref[0] * acc_ref[...]


def _moe_call(f, w2, x, g2, lw, tm):
    B, N, D = x.shape
    GW = EPG * D_EXPERT
    row = lambda b, i, g: (b, i, 0)
    return pl.pallas_call(
        _moe_kernel,
        grid=(B, N // tm, N_EGROUPS),
        in_specs=[
            pl.BlockSpec((1, tm, D), row),
            pl.BlockSpec((1, tm, 2 * N_EXPERTS), row),
            pl.BlockSpec((1, tm, D), row),
            pl.BlockSpec((1, 1, D), lambda b, i, g: (b, 0, 0)),
            pl.BlockSpec((1, 2 * N_EXPERTS, GW), lambda b, i, g: (g, 0, 0)),
            pl.BlockSpec((1, D, GW), lambda b, i, g: (g, 0, 0)),
            pl.BlockSpec((1, D, GW), lambda b, i, g: (g, 0, 0)),
            pl.BlockSpec((1, GW, D), lambda b, i, g: (g, 0, 0)),
        ],
        out_specs=pl.BlockSpec((1, tm, D), row),
        out_shape=jax.ShapeDtypeStruct((B, N, D), F32),
        scratch_shapes=[pltpu.VMEM((tm, D), F32)],
        compiler_params=_cparams(("parallel", "parallel", "arbitrary")),
        name="moe_experts",
    )(f, w2, x, g2, lw["expand"], lw["wgate"], lw["wup"], lw["wdown"])


def _rope_partner():
    p = np.arange(SLOT)
    for j in range(ROPE_HALF):
        for base in (QK_NOPE, QK_NOPE + ROPE_AXIS):
            p[base + j] = base + ROPE_HALF + j
            p[base + ROPE_HALF + j] = base + j
    return p


def _rope_tables(n):
    rows = n // GRID_W
    row = jnp.repeat(jnp.arange(rows, dtype=F32), GRID_W)
    col = jnp.tile(jnp.arange(GRID_W, dtype=F32), rows)
    freqs = ROPE_BASE ** (-jnp.arange(ROPE_HALF, dtype=F32) / ROPE_HALF)
    ar, ac = row[:, None] * freqs, col[:, None] * freqs
    one = jnp.ones((n, QK_NOPE), F32)
    zero = jnp.zeros((n, QK_NOPE), F32)
    pad1 = jnp.ones((n, SLOT - QK_DIM), F32)
    pad0 = jnp.zeros((n, SLOT - QK_DIM), F32)
    cos_t = jnp.concatenate([one, jnp.cos(ar), jnp.cos(ar), jnp.cos(ac), jnp.cos(ac), pad1], axis=1)
    sin_t = jnp.concatenate([zero, -jnp.sin(ar), jnp.sin(ar), -jnp.sin(ac), jnp.sin(ac), pad0], axis=1)
    return cos_t, sin_t


def _slot_cols(w, width):
    kdim = w.shape[0]
    w = w.reshape(kdim, N_HEADS, width)
    return jnp.pad(w, ((0, 0), (0, 0), (0, SLOT - width))).reshape(kdim, HSLOTS)


def _layer_weights(l, p, mf_all):
    partner = _rope_partner()
    w_in = p["w_in"][l]
    kr_cols = w_in[:, Q_LORA + KV_LORA:Q_LORA + KV_LORA + QK_ROPE]
    kr_slot = jnp.pad(kr_cols, ((0, 0), (QK_NOPE, SLOT - QK_DIM)))
    kr_swap = jnp.where((np.arange(SLOT) >= QK_NOPE) & (np.arange(SLOT) < QK_DIM), kr_slot[:, partner], 0.0)
    win = jnp.concatenate([w_in[:, :Q_LORA + KV_LORA], kr_slot, kr_swap, w_in[:, Q_LORA + KV_LORA + QK_ROPE:]], axis=1)

    rope_lane = (np.arange(SLOT) >= QK_NOPE) & (np.arange(SLOT) < QK_DIM)
    wq = _slot_cols(p["w_q_b"][l], QK_DIM)
    wq3 = wq.reshape(Q_LORA, N_HEADS, SLOT)
    wqs = jnp.where(rope_lane, wq3[:, :, partner], 0.0).reshape(Q_LORA, HSLOTS)

    wkv = p["w_kv_b"][l].reshape(KV_LORA, N_HEADS, QK_NOPE + V_DIM)
    wk = jnp.pad(wkv[:, :, :QK_NOPE], ((0, 0), (0, 0), (0, SLOT - QK_NOPE))).reshape(KV_LORA, HSLOTS)
    wv = wkv[:, :, QK_NOPE:].reshape(KV_LORA, MLA_WIDTH)

    def gains(g, scale):
        gp = jnp.pad(g, (0, SLOT - QK_DIM))
        gs = jnp.where(rope_lane, gp[partner], 0.0)
        return (jnp.tile(gp, N_HEADS) * scale)[None, :], (jnp.tile(gs, N_HEADS) * scale)[None, :]

    gq, gqs = gains(p["q_norm"][l], QK_DIM ** -0.5 * math.log2(math.e))
    gk, gks = gains(p["k_norm"][l], 1.0)

    def group_cols(w):
        return w.reshape(N_EGROUPS, EPG, D_MODEL, D_EXPERT).transpose(0, 2, 1, 3).reshape(
            N_EGROUPS, D_MODEL, EPG * D_EXPERT).astype(BF16)

    return {
        "gmix": p["norm_mix"][l][None, :],
        "gffn": p["norm_ffn"][l][None, :],
        "win": win.astype(BF16),
        "qa": p["q_a_norm"][l][None, :],
        "wq": wq.astype(BF16),
        "wqs": wqs.astype(BF16),
        "kva": p["kv_a_norm"][l][None, :],
        "wk": wk.astype(BF16),
        "wv": wv.astype(BF16),
        "gq": gq, "gqs": gqs, "gk": gk, "gks": gks,
        "mf": mf_all[l],
        "wout": p["w_out"][l].astype(BF16),
        "wgate": group_cols(p["w_gate"][l]),
        "wup": group_cols(p["w_up"][l]),
        "wdown": p["w_down"][l].reshape(N_EGROUPS, EPG * D_EXPERT, D_MODEL).astype(BF16),
    }


def _const_tables():
    lane_head = np.arange(HSLOTS) // SLOT
    esum = (lane_head[:, None] == lane_head[None, :]).astype(np.float32)
    ex = np.zeros((N_EGROUPS, 2 * N_EXPERTS, EPG * D_EXPERT), np.float32)
    for g in range(N_EGROUPS):
        for i in range(EPG):
            ex[g, EPG * g + i, i * D_EXPERT:(i + 1) * D_EXPERT] = 1.0
            ex[g, N_EXPERTS + EPG * g + i, i * D_EXPERT:(i + 1) * D_EXPERT] = 1.0
    return jnp.asarray(esum, BF16), jnp.asarray(ex, BF16)


def _split_hi_lo(w):
    wt = jnp.swapaxes(w, 1, 2)
    hi = wt.astype(BF16)
    lo = (wt - hi.astype(F32)).astype(BF16)
    return jnp.concatenate([hi, lo], axis=-1)


def _mod6(mod_l, rows, B):
    m = mod_l[jnp.asarray(rows)]
    return [m[:, i * D_MODEL:(i + 1) * D_MODEL][:, None, :] for i in range(6)]


def kernel(x, c, ctx, c_ctx, w_ada, b_ada, norm_mix, norm_ffn, w_in, q_a_norm, w_q_b, kv_a_norm, w_kv_b, q_norm,
           k_norm, w_fourier, w_out, router_w, router_bias, w_gate, w_up, w_down):
    B, S, D = x.shape
    n_ctx = ctx.shape[1]
    depth = w_ada.shape[0]
    p = dict(w_in=w_in, q_a_norm=q_a_norm, w_q_b=w_q_b, kv_a_norm=kv_a_norm, w_kv_b=w_kv_b, q_norm=q_norm,
             k_norm=k_norm, norm_mix=norm_mix, norm_ffn=norm_ffn, w_out=w_out, w_gate=w_gate, w_up=w_up,
             w_down=w_down)

    c8 = jnp.concatenate([c, c_ctx[None, :], jnp.zeros((8 - B - 1, D), F32)], axis=0)
    mod = _ada_call(c8, w_ada, b_ada)
    mf_all = _fmat_call(w_fourier)
    esum, expand = _const_tables()
    cos_l, sin_l = _rope_tables(S)
    cos_c = jnp.ones((n_ctx, SLOT), F32)
    sin_c = jnp.zeros((n_ctx, SLOT), F32)
    rwt = router_w.T
    rb = router_bias[:, None]

    tm_lat = 512
    tm_ctx = min(256, n_ctx)
    for l in range(depth):
        last = l == depth - 1
        lw = _layer_weights(l, p, mf_all)
        lw["esum"] = esum
        lw["expand"] = expand
        sh1, sc1, g1, sh2, sc2, g2 = _mod6(mod[l], list(range(B)), B)
        csh1, csc1, cg1, csh2, csc2, cg2 = _mod6(mod[l], [B] * B, B)

        q_c, k_c, v_c, wri_c = _premix_call(ctx, csh1, csc1, lw, cos_c, sin_c, tm_ctx)
        q_l, k_l, v_l, wri_l = _premix_call(x, sh1, sc1, lw, cos_l, sin_l, tm_lat)
        k_all = jnp.concatenate([k_c, k_l], axis=1)
        v_all = jnp.concatenate([v_c, v_l], axis=1)
        bounded = _scores_bounded(q_norm[l], k_norm[l])
        o_l = _attention(q_l, k_all, v_all, 512, bounded)
        z_l = _fft_call(wri_l)
        x_mid, f_l, wd_l = _postmix_call(o_l, z_l, x, g1, sh2, sc2, lw, rwt, rb, tm_lat)
        x = _moe_call(f_l, _split_hi_lo(wd_l), x_mid, g2, lw, tm_lat)
        if not last:
            o_c = _attention(q_c, k_c, v_c, tm_ctx, bounded)
            z_c = _dft_small_call(wri_c)
            c_mid, f_c, wd_c = _postmix_call(o_c, z_c, ctx, cg1, csh2, csc2, lw, rwt, rb, tm_ctx)
            ctx = _moe_call(f_c, _split_hi_lo(wd_c), c_mid, cg2, lw, tm_ctx)
    return x
```

```python
import functools
import math

import numpy as np
import jax
import jax.numpy as jnp
from jax import lax
from jax.experimental import pallas as pl
from jax.experimental.pallas import tpu as pltpu

F32 = jnp.float32
BF16 = jnp.bfloat16
HIGHEST = lax.Precision.HIGHEST

D_MODEL = 1024
N_HEADS = 8
QK_NOPE = 64
QK_ROPE = 32
QK_DIM = QK_NOPE + QK_ROPE
V_DIM = 64
Q_LORA = 256
KV_LORA = 128
ROPE_AXIS = QK_ROPE // 2
ROPE_HALF = ROPE_AXIS // 2
ROPE_BASE = 10000.0
GRID_W = 64
F_GROUPS = 4
F_GDIM = 128
F_WIDTH = F_GROUPS * F_GDIM
MLA_WIDTH = N_HEADS * V_DIM
N_EXPERTS = 16
N_EGROUPS = 4
EPG = N_EXPERTS // N_EGROUPS
D_EXPERT = 256
EPS = 1e-6
ROUTE_ROWS = 16

LANES = 128
SLOT = LANES
HSLOTS = N_HEADS * SLOT
VMEM_LIMIT = 56 * 1024 * 1024

C_CQ = 0
C_CKV = C_CQ + Q_LORA
C_KR = C_CKV + KV_LORA
C_KRS = C_KR + SLOT
C_F = C_KRS + SLOT
IN_COLS = C_F + F_WIDTH


def _cparams(sem):
    return pltpu.CompilerParams(dimension_semantics=sem, vmem_limit_bytes=VMEM_LIMIT)


def _full(shape):
    n = len(shape)
    return pl.BlockSpec(shape, lambda *_: (0,) * n)


def _ada_kernel(c_ref, w_ref, b_ref, o_ref):
    c = c_ref[...]
    s = c * jax.nn.sigmoid(c)
    o_ref[0] = jnp.dot(s, w_ref[0], preferred_element_type=F32, precision=HIGHEST) + b_ref[0]


def _ada_call(c8, w_ada, b_ada):
    L, D, W = w_ada.shape
    tn = 1536
    return pl.pallas_call(
        _ada_kernel,
        grid=(L, W // tn),
        in_specs=[
            pl.BlockSpec((8, D), lambda l, j: (0, 0)),
            pl.BlockSpec((1, D, tn), lambda l, j: (l, 0, j)),
            pl.BlockSpec((1, 1, tn), lambda l, j: (l, 0, j)),
        ],
        out_specs=pl.BlockSpec((1, 8, tn), lambda l, j: (l, 0, j)),
        out_shape=jax.ShapeDtypeStruct((L, 8, W), F32),
        compiler_params=_cparams(("parallel", "parallel")),
        name="adaln_mod",
    )(c8, w_ada, b_ada.reshape(L, 1, W))


def _fmat_kernel(cc_ref, sc_ref, wf_ref, o_ref):
    o_ref[...] = jnp.zeros(o_ref.shape, o_ref.dtype)
    for g in range(F_GROUPS):
        wf = wf_ref[0, g]
        mr = jnp.dot(cc_ref[...], wf, preferred_element_type=F32, precision=HIGHEST)
        mi = jnp.dot(sc_ref[...], wf, preferred_element_type=F32, precision=HIGHEST)
        r0 = g * F_GDIM
        o_ref[0, r0:r0 + F_GDIM, r0:r0 + F_GDIM] = mr.astype(o_ref.dtype)
        o_ref[0, r0:r0 + F_GDIM, F_WIDTH + r0:F_WIDTH + r0 + F_GDIM] = mi.astype(o_ref.dtype)


def _fmat_call(w_fourier):
    L = w_fourier.shape[0]
    c = np.arange(F_GDIM)
    ang = 2.0 * np.pi * np.outer(c, c) / F_GDIM
    cc = jnp.asarray((np.cos(ang) / math.sqrt(F_GDIM)).astype(np.float32))
    sc = jnp.asarray((-np.sin(ang) / math.sqrt(F_GDIM)).astype(np.float32))
    return pl.pallas_call(
        _fmat_kernel,
        grid=(L,),
        in_specs=[
            _full((F_GDIM, F_GDIM)),
            _full((F_GDIM, F_GDIM)),
            pl.BlockSpec((1, F_GROUPS, F_GDIM, F_GDIM), lambda l: (l, 0, 0, 0)),
        ],
        out_specs=pl.BlockSpec((1, F_WIDTH, 2 * F_WIDTH), lambda l: (l, 0, 0)),
        out_shape=jax.ShapeDtypeStruct((L, F_WIDTH, 2 * F_WIDTH), BF16),
        compiler_params=_cparams(("parallel",)),
        name="fourier_chan_mats",
    )(cc, sc, w_fourier)


def _rms(x):
    return x * lax.rsqrt(jnp.mean(x * x, axis=-1, keepdims=True) + EPS)


def _tile8(a):
    return jnp.concatenate([a] * N_HEADS, axis=-1)


def _head_sums(x2, e2):
    w = e2.shape[0]
    xb = x2.astype(BF16)
    return jnp.concatenate(
        [jnp.dot(xb[:, c * w:(c + 1) * w], e2, preferred_element_type=F32) for c in range(HSLOTS // w)], axis=-1)


def _premix_kernel(x_ref, sh_ref, sc_ref, gn_ref, win_ref, qa_ref, wq_ref, wqs_ref, kva_ref, wk_ref, wv_ref,
                   e_ref, cos_ref, sin_ref, gq_ref, gqs_ref, gk_ref, gks_ref, mf_ref,
                   q_ref, k_ref, v_ref, wri_ref):
    x = x_ref[0]
    h = _rms(x) * gn_ref[...] * (1.0 + sc_ref[0]) + sh_ref[0]
    proj = jnp.dot(h.astype(BF16), win_ref[...], preferred_element_type=F32)
    cq = proj[:, C_CQ:C_CKV]
    ckv = proj[:, C_CKV:C_KR]
    kr = proj[:, C_KR:C_KRS]
    krs = proj[:, C_KRS:C_F]
    f = proj[:, C_F:IN_COLS]

    cos8 = _tile8(cos_ref[...])
    sin8 = _tile8(sin_ref[...])
    inv_d = 1.0 / QK_DIM

    cqn = (_rms(cq) * qa_ref[...]).astype(BF16)
    qr = jnp.dot(cqn, wq_ref[...], preferred_element_type=F32)
    qs = jnp.dot(cqn, wqs_ref[...], preferred_element_type=F32)
    ssq = _head_sums(qr * qr, e_ref[...])
    rq = lax.rsqrt(ssq * inv_d + EPS)
    q = rq * (qr * (gq_ref[...] * cos8) + qs * (gqs_ref[...] * sin8))
    q_ref[0] = q.astype(q_ref.dtype)

    ckvn = (_rms(ckv) * kva_ref[...]).astype(BF16)
    kraw = jnp.dot(ckvn, wk_ref[...], preferred_element_type=F32) + _tile8(kr)
    ssk = _head_sums(kraw * kraw, e_ref[...])
    rk = lax.rsqrt(ssk * inv_d + EPS)
    k = rk * (kraw * (gk_ref[...] * cos8) + _tile8(krs) * (gks_ref[...] * sin8))
    k_ref[0] = k.astype(k_ref.dtype)
    v_ref[0] = jnp.dot(ckvn, wv_ref[...], preferred_element_type=F32).astype(v_ref.dtype)

    w = jnp.dot(f.astype(BF16), mf_ref[...], preferred_element_type=F32)
    wri_ref[0, 0] = w[:, :F_WIDTH].astype(wri_ref.dtype)
    wri_ref[0, 1] = w[:, F_WIDTH:].astype(wri_ref.dtype)


def _premix_call(x, sh, sc, lw, cos_t, sin_t, tm):
    B, N, D = x.shape
    row = lambda b, i: (b, i, 0)
    vec = lambda b, i: (b, 0, 0)
    in_specs = [
        pl.BlockSpec((1, tm, D), row),
        pl.BlockSpec((1, 1, D), vec),
        pl.BlockSpec((1, 1, D), vec),
        _full((1, D)),
        _full((D, IN_COLS)),
        _full((1, Q_LORA)),
        _full((Q_LORA, HSLOTS)),
        _full((Q_LORA, HSLOTS)),
        _full((1, KV_LORA)),
        _full((KV_LORA, HSLOTS)),
        _full((KV_LORA, MLA_WIDTH)),
        _full((2 * SLOT, 2 * SLOT)),
        pl.BlockSpec((tm, SLOT), lambda b, i: (i, 0)),
        pl.BlockSpec((tm, SLOT), lambda b, i: (i, 0)),
        _full((1, HSLOTS)),
        _full((1, HSLOTS)),
        _full((1, HSLOTS)),
        _full((1, HSLOTS)),
        _full((F_WIDTH, 2 * F_WIDTH)),
    ]
    out_specs = [
        pl.BlockSpec((1, tm, HSLOTS), row),
        pl.BlockSpec((1, tm, HSLOTS), row),
        pl.BlockSpec((1, tm, MLA_WIDTH), row),
        pl.BlockSpec((1, 2, tm, F_WIDTH), lambda b, i: (b, 0, i, 0)),
    ]
    out_shape = [
        jax.ShapeDtypeStruct((B, N, HSLOTS), BF16),
        jax.ShapeDtypeStruct((B, N, HSLOTS), BF16),
        jax.ShapeDtypeStruct((B, N, MLA_WIDTH), BF16),
        jax.ShapeDtypeStruct((B, 2, N, F_WIDTH), BF16),
    ]
    return pl.pallas_call(
        _premix_kernel,
        grid=(B, N // tm),
        in_specs=in_specs,
        out_specs=out_specs,
        out_shape=out_shape,
        compiler_params=_cparams(("parallel", "parallel")),
        name="premix_proj",
    )(x, sh, sc, lw["gmix"], lw["win"], lw["qa"], lw["wq"], lw["wqs"], lw["kva"], lw["wk"], lw["wv"],
      lw["esum"], cos_t, sin_t, lw["gq"], lw["gqs"], lw["gk"], lw["gks"], lw["mf"])


def _attn_kernel(q_ref, k_ref, v_ref, ot_ref, vt_scr, *, n_kt, bounded):
    tq = q_ref.shape[1]
    tk = ATTN_TK

    @pl.when(pl.program_id(2) == 0)
    def _():
        odd = (pl.program_id(1) % 2) == 1
        ones = jnp.ones((VT_ROWS - V_DIM, tk), BF16)
        for t in range(n_kt):
            vt_pair = v_ref[0, t * tk:(t + 1) * tk, :].T
            vt_scr[t, :V_DIM, :] = jnp.where(odd, vt_pair[V_DIM:], vt_pair[:V_DIM])
            vt_scr[t, V_DIM:, :] = ones

    qt = q_ref[0].T

    def scores(t):
        off = t * tk if isinstance(t, int) else pl.multiple_of(t * tk, tk)
        return jnp.dot(k_ref[0, pl.ds(off, tk), :], qt, preferred_element_type=F32)

    def pv(t, p):
        return jnp.dot(vt_scr[t], p.astype(BF16), preferred_element_type=F32)

    if bounded:
        acc = jnp.zeros((VT_ROWS, tq), F32)
        s = scores(0)
        for t in range(n_kt):
            s_next = scores(t + 1) if t + 1 < n_kt else None
            acc = acc + pv(t, jnp.exp2(s))
            s = s_next
    else:
        unroll = _attn_unroll(n_kt)

        def body(g, carry):
            m, acc, s = carry
            for u in range(unroll):
                t = g * unroll + u
                s_next = scores(jnp.minimum(t + 1, n_kt - 1))
                m_new = jnp.maximum(m, jnp.max(s, axis=0, keepdims=True))
                acc = jnp.exp2(m - m_new) * acc + pv(t, jnp.exp2(s - m_new))
                m, s = m_new, s_next
            return m, acc, s

        init = (jnp.full((1, tq), -1e30, F32), jnp.zeros((VT_ROWS, tq), F32), scores(0))
        _, acc, _ = lax.fori_loop(0, n_kt // unroll, body, init)
    ot_ref[0] = (acc[:V_DIM] / acc[V_DIM:V_DIM + 1]).astype(ot_ref.dtype)


ATTN_TK = 256
VT_ROWS = V_DIM + 16
ATTN_SCORE_BOUND = 60.0


def _attn_unroll(n_kt):
    for u in (13, 9, 8, 5, 4, 3, 2):
        if n_kt % u == 0:
            return u
    return 1


def _attn_call(q, k, v, tq, bounded):
    B, Nq, _ = q.shape
    Nk = k.shape[1]
    n_kt = Nk // ATTN_TK
    kern = functools.partial(_attn_kernel, n_kt=n_kt, bounded=bounded)
    return pl.pallas_call(
        kern,
        grid=(B, N_HEADS, Nq // tq),
        in_specs=[
            pl.BlockSpec((1, tq, SLOT), lambda b, h, i: (b, i, h)),
            pl.BlockSpec((1, Nk, SLOT), lambda b, h, i: (b, 0, h)),
            pl.BlockSpec((1, Nk, 2 * V_DIM), lambda b, h, i: (b, 0, h // 2)),
        ],
        out_specs=pl.BlockSpec((1, V_DIM, tq), lambda b, h, i: (b, h, i)),
        out_shape=jax.ShapeDtypeStruct((B, MLA_WIDTH, Nq), BF16),
        scratch_shapes=[pltpu.VMEM((n_kt, VT_ROWS, ATTN_TK), BF16)],
        compiler_params=_cparams(("arbitrary", "arbitrary", "arbitrary")),
        name="mla_attention_bounded" if bounded else "mla_attention_online",
    )(q, k, v)


def _attention(q, k, v, tq, scores_bounded):
    return lax.cond(scores_bounded,
                    lambda a, b, c: _attn_call(a, b, c, tq, True),
                    lambda a, b, c: _attn_call(a, b, c, tq, False),
                    q, k, v)


def _scores_bounded(q_gain, k_gain):
    bound = math.sqrt(QK_DIM) * math.log2(math.e) * jnp.max(jnp.abs(q_gain)) * jnp.max(jnp.abs(k_gain))
    return bound <= ATTN_SCORE_BOUND


def _fft1_kernel(f1_ref, tr_ref, ti_ref, w_ref, y_ref, *, n1, n2_per):
    y = jnp.dot(f1_ref[...], w_ref[0], preferred_element_type=F32)
    yr = y[:n1]
    yi = y[n1:]
    for t in range(n2_per):
        tr = tr_ref[:, t * LANES:(t + 1) * LANES]
        ti = ti_ref[:, t * LANES:(t + 1) * LANES]
        tr4 = jnp.concatenate([tr] * F_GROUPS, axis=-1)
        ti4 = jnp.concatenate([ti] * F_GROUPS, axis=-1)
        a = yr[:, t * F_WIDTH:(t + 1) * F_WIDTH]
        b = yi[:, t * F_WIDTH:(t + 1) * F_WIDTH]
        y_ref[0, :n1, t * F_WIDTH:(t + 1) * F_WIDTH] = (a * tr4 - b * ti4).astype(y_ref.dtype)
        y_ref[0, n1:, t * F_WIDTH:(t + 1) * F_WIDTH] = (a * ti4 + b * tr4).astype(y_ref.dtype)


def _fft2_kernel(c2_ref, s2_ref, y_ref, o_ref, *, kc):
    for i in range(kc):
        re = jnp.dot(c2_ref[...], y_ref[0, 0, i], preferred_element_type=F32)
        im = jnp.dot(s2_ref[...], y_ref[0, 1, i], preferred_element_type=F32)
        o_ref[0, :, i * F_WIDTH:(i + 1) * F_WIDTH] = (re + im).astype(o_ref.dtype)


def _fft_consts(n):
    n2 = LANES
    n1 = n // n2
    a1 = 2.0 * np.pi * np.outer(np.arange(n1), np.arange(n1)) / n1
    c1, s1 = np.cos(a1), np.sin(a1)
    f1 = np.block([[c1, s1], [-s1, c1]]).astype(np.float32)
    at = 2.0 * np.pi * np.outer(np.arange(n1), np.arange(n2)) / n
    tr = np.repeat(np.cos(at), LANES, axis=1).astype(np.float32)
    ti = np.repeat(-np.sin(at), LANES, axis=1).astype(np.float32)
    a2 = 2.0 * np.pi * np.outer(np.arange(n2), np.arange(n2)) / n2
    scale = 1.0 / math.sqrt(n)
    c2 = (np.cos(a2) * scale).astype(np.float32)
    s2 = (np.sin(a2) * scale).astype(np.float32)
    return n1, n2, f1, tr, ti, c2, s2


def _fft_call(wri):
    B, _, N, _ = wri.shape
    n1, n2, f1, tr, ti, c2, s2 = _fft_consts(N)
    cols = n2 * F_WIDTH
    n2_per = 4
    tc = n2_per * F_WIDTH
    y = pl.pallas_call(
        functools.partial(_fft1_kernel, n1=n1, n2_per=n2_per),
        grid=(B, cols // tc),
        in_specs=[
            _full((2 * n1, 2 * n1)),
            pl.BlockSpec((n1, n2_per * LANES), lambda b, j: (0, j)),
            pl.BlockSpec((n1, n2_per * LANES), lambda b, j: (0, j)),
            pl.BlockSpec((1, 2 * n1, tc), lambda b, j: (b, 0, j)),
        ],
        out_specs=pl.BlockSpec((1, 2 * n1, tc), lambda b, j: (b, 0, j)),
        out_shape=jax.ShapeDtypeStruct((B, 2 * n1, cols), BF16),
        compiler_params=_cparams(("parallel", "parallel")),
        name="fft_stage1",
    )(jnp.asarray(f1, BF16), jnp.asarray(tr), jnp.asarray(ti), wri.reshape(B, 2 * n1, cols))
    kc = min(8, n1)
    out = pl.pallas_call(
        functools.partial(_fft2_kernel, kc=kc),
        grid=(B, n1 // kc),
        in_specs=[
            _full((n2, n2)),
            _full((n2, n2)),
            pl.BlockSpec((1, 2, kc, n2, F_WIDTH), lambda b, j: (b, 0, j, 0, 0)),
        ],
        out_specs=pl.BlockSpec((1, n2, kc * F_WIDTH), lambda b, j: (b, 0, j)),
        out_shape=jax.ShapeDtypeStruct((B, n2, n1 * F_WIDTH), BF16),
        compiler_params=_cparams(("parallel", "parallel")),
        name="fft_stage2",
    )(jnp.asarray(c2, BF16), jnp.asarray(s2, BF16), y.reshape(B, 2, n1, n2, F_WIDTH))
    return out.reshape(B, N, F_WIDTH)


def _dft_small_kernel(c_ref, s_ref, w_ref, o_ref):
    re = jnp.dot(c_ref[...], w_ref[0, 0], preferred_element_type=F32)
    im = jnp.dot(s_ref[...], w_ref[0, 1], preferred_element_type=F32)
    o_ref[0] = (re + im).astype(o_ref.dtype)


def _dft_small_call(wri):
    B, _, N, _ = wri.shape
    a = 2.0 * np.pi * np.outer(np.arange(N), np.arange(N)) / N
    c = (np.cos(a) / math.sqrt(N)).astype(np.float32)
    s = (np.sin(a) / math.sqrt(N)).astype(np.float32)
    return pl.pallas_call(
        _dft_small_kernel,
        grid=(B,),
        in_specs=[_full((N, N)), _full((N, N)), pl.BlockSpec((1, 2, N, F_WIDTH), lambda b: (b, 0, 0, 0))],
        out_specs=pl.BlockSpec((1, N, F_WIDTH), lambda b: (b, 0, 0)),
        out_shape=jax.ShapeDtypeStruct((B, N, F_WIDTH), BF16),
        compiler_params=_cparams(("parallel",)),
        name="dft_small",
    )(jnp.asarray(c, BF16), jnp.asarray(s, BF16), wri)


def _router_rows(logits, bias):
    t = logits.shape[1]
    s = jax.nn.sigmoid(logits)
    b = s + bias
    srow = [s[e:e + 1, :] for e in range(N_EXPERTS)]
    brow = [b[e:e + 1, :] for e in range(N_EXPERTS)]
    gscore = []
    for g in range(N_EGROUPS):
        b0, b1, b2, b3 = brow[EPG * g:EPG * g + EPG]
        m1, n1 = jnp.maximum(b0, b1), jnp.minimum(b0, b1)
        m2, n2 = jnp.maximum(b2, b3), jnp.minimum(b2, b3)
        top1 = jnp.maximum(m1, m2)
        top2 = jnp.maximum(jnp.minimum(m1, m2), jnp.maximum(n1, n2))
        gscore.append(top1 + top2)
    best = gscore[0]
    gi = jnp.zeros((1, t), jnp.int32)
    for g in range(1, N_EGROUPS):
        take = gscore[g] > best
        gi = jnp.where(take, g, gi)
        best = jnp.where(take, gscore[g], best)
    ib, isc = [], []
    for i in range(EPG):
        vb, vs = brow[i], srow[i]
        for g in range(1, N_EGROUPS):
            sel = gi == g
            vb = jnp.where(sel, brow[EPG * g + i], vb)
            vs = jnp.where(sel, srow[EPG * g + i], vs)
        ib.append(vb)
        isc.append(vs)
    w = []
    for i in range(EPG):
        rank = jnp.zeros((1, t), jnp.int32)
        for j in range(EPG):
            if j == i:
                continue
            ahead = (ib[j] > ib[i]) | ((ib[j] == ib[i]) & (j < i))
            rank = rank + ahead.astype(jnp.int32)
        w.append(jnp.where(rank < 2, isc[i], 0.0))
    tot = (w[0] + w[1]) + (w[2] + w[3])
    inv = 1.0 / tot
    ridx = lax.broadcasted_iota(jnp.int32, (ROUTE_ROWS, t), 0)
    out = jnp.where(ridx == 2 * EPG, gi.astype(F32), 0.0)
    for i in range(EPG):
        out = jnp.where((ridx == i) | (ridx == EPG + i), w[i] * inv, out)
    return out


def _postmix_kernel(ot_ref, z_ref, x_ref, g1_ref, sh_ref, sc_ref, gn_ref, wo_ref, rw_ref, rb_ref,
                    xo_ref, f_ref, wd_ref):
    mix = lax.dot_general(ot_ref[0], wo_ref[:MLA_WIDTH, :], (((0,), (0,)), ((), ())), preferred_element_type=F32)
    mix = mix + jnp.dot(z_ref[0], wo_ref[MLA_WIDTH:, :], preferred_element_type=F32)
    xn = x_ref[0] + g1_ref[0] * mix
    xo_ref[0] = xn
    f = _rms(xn) * gn_ref[...] * (1.0 + sc_ref[0]) + sh_ref[0]
    f_ref[0] = f.astype(f_ref.dtype)
    logits = lax.dot_general(rw_ref[...], f, (((1,), (1,)), ((), ())), preferred_element_type=F32,
                             precision=HIGHEST)
    wd_ref[0] = _router_rows(logits, rb_ref[...])


def _postmix_call(o, z, x, g1, sh, sc, lw, rwt, rb, tm):
    B, N, D = x.shape
    row = lambda b, i: (b, i, 0)
    vec = lambda b, i: (b, 0, 0)
    return pl.pallas_call(
        _postmix_kernel,
        grid=(B, N // tm),
        in_specs=[
            pl.BlockSpec((1, MLA_WIDTH, tm), lambda b, i: (b, 0, i)),
            pl.BlockSpec((1, tm, F_WIDTH), row),
            pl.BlockSpec((1, tm, D), row),
            pl.BlockSpec((1, 1, D), vec),
            pl.BlockSpec((1, 1, D), vec),
            pl.BlockSpec((1, 1, D), vec),
            _full((1, D)),
            _full((D, D)),
            _full((N_EXPERTS, D)),
            _full((N_EXPERTS, 1)),
        ],
        out_specs=[
            pl.BlockSpec((1, tm, D), row),
            pl.BlockSpec((1, tm, D), row),
            pl.BlockSpec((1, ROUTE_ROWS, tm), lambda b, i: (b, 0, i)),
        ],
        out_shape=[
            jax.ShapeDtypeStruct((B, N, D), F32),
            jax.ShapeDtypeStruct((B, N, D), BF16),
            jax.ShapeDtypeStruct((B, ROUTE_ROWS, N), F32),
        ],
        compiler_params=_cparams(("parallel", "parallel")),
        name="postmix_router",
    )(o, z, x, g1, sh, sc, lw["gffn"], lw["wout"], rwt, rb)


def _expert_mlp(xb, wexp, g, wgate_ref, wup_ref, wdown_ref):
    hg = jnp.dot(xb, wgate_ref[g], preferred_element_type=F32)
    hu = jnp.dot(xb, wup_ref[g], preferred_element_type=F32)
    h = (hg * jax.nn.sigmoid(hg)) * hu * wexp
    return jnp.dot(h.astype(BF16), wdown_ref[g], preferred_element_type=F32)


def _moe_kernel(ovf_ref, f_ref, rt_ref, x_ref, g2_ref, tri_ref, ex_ref, wgate_ref, wup_ref, wdown_ref, xo_ref, *, cap):
    tm = f_ref.shape[0]
    f = f_ref[...]
    rt = rt_ref[0]
    gid = rt[2 * EPG:2 * EPG + 1, :]
    r16 = lax.broadcasted_iota(jnp.int32, (ROUTE_ROWS, tm), 0)
    hi = rt.astype(BF16).astype(F32)
    whl = jnp.where(r16 < EPG, hi, jnp.where(r16 < 2 * EPG, rt - hi, 0.0)).astype(BF16)
    overflow = ovf_ref[pl.program_id(0)] != 0

    @pl.when(jnp.logical_not(overflow))
    def _():
        onehot = (r16.astype(F32) == gid).astype(BF16)
        before = jnp.dot(onehot, tri_ref[...], preferred_element_type=F32)
        rank = jnp.sum(onehot.astype(F32) * before, axis=0, keepdims=True)
        slot = lax.broadcasted_iota(jnp.int32, (cap, tm), 0).astype(F32)
        y = jnp.zeros((tm, D_MODEL), F32)
        for g in range(N_EGROUPS):
            sel = ((slot == rank) & (gid == g)).astype(BF16)
            xg = jnp.dot(sel, f, preferred_element_type=F32).astype(BF16)
            wr = lax.dot_general(sel, whl, (((1,), (1,)), ((), ())), preferred_element_type=F32).astype(BF16)
            wexp = jnp.dot(wr, ex_ref[...], preferred_element_type=F32)
            yg = _expert_mlp(xg, wexp, g, wgate_ref, wup_ref, wdown_ref)
            y = y + lax.dot_general(sel, yg.astype(BF16), (((0,), (0,)), ((), ())), preferred_element_type=F32)
        xo_ref[...] = x_ref[...] + g2_ref[0] * y

    @pl.when(overflow)
    def _():
        y = jnp.zeros((tm, D_MODEL), F32)
        for g in range(N_EGROUPS):
            wm = jnp.where(gid == g, whl, jnp.zeros_like(whl))
            wexp = lax.dot_general(wm, ex_ref[...], (((0,), (0,)), ((), ())), preferred_element_type=F32)
            y = y + _expert_mlp(f, wexp, g, wgate_ref, wup_ref, wdown_ref)
        xo_ref[...] = x_ref[...] + g2_ref[0] * y


MOE_CAP = 160


def _moe_call(f, rt, x, g2, lw, tm):
    B, N, D = x.shape
    GW = EPG * D_EXPERT
    tpb = N // tm
    n_tiles = B * tpb
    cap = min(MOE_CAP, tm)
    gid = rt[:, 2 * EPG, :].reshape(n_tiles, tm)
    counts = jnp.stack([jnp.sum(gid == g, axis=1) for g in range(N_EGROUPS)], axis=1)
    ovf = (jnp.max(counts, axis=1) > cap).astype(jnp.int32)
    tri = jnp.asarray(np.triu(np.ones((tm, tm), np.float32), 1), BF16)
    resident = dict(pipeline_mode=pl.Buffered(1))
    grid_spec = pltpu.PrefetchScalarGridSpec(
        num_scalar_prefetch=1,
        grid=(n_tiles,),
        in_specs=[
            pl.BlockSpec((tm, D), lambda i, o: (i, 0)),
            pl.BlockSpec((1, ROUTE_ROWS, tm), lambda i, o: (i // tpb, 0, i % tpb)),
            pl.BlockSpec((tm, D), lambda i, o: (i, 0)),
            pl.BlockSpec((1, 1, D), lambda i, o: (i // tpb, 0, 0)),
            pl.BlockSpec((tm, tm), lambda i, o: (0, 0), **resident),
            pl.BlockSpec((ROUTE_ROWS, GW), lambda i, o: (0, 0), **resident),
            pl.BlockSpec((N_EGROUPS, D, GW), lambda i, o: (0, 0, 0), **resident),
            pl.BlockSpec((N_EGROUPS, D, GW), lambda i, o: (0, 0, 0), **resident),
            pl.BlockSpec((N_EGROUPS, GW, D), lambda i, o: (0, 0, 0), **resident),
        ],
        out_specs=pl.BlockSpec((tm, D), lambda i, o: (i, 0)),
    )
    out = pl.pallas_call(
        functools.partial(_moe_kernel, cap=cap),
        grid_spec=grid_spec,
        out_shape=jax.ShapeDtypeStruct((B * N, D), F32),
        compiler_params=_cparams(("arbitrary",)),
        name="moe_experts",
    )(ovf, f.reshape(B * N, D), rt, x.reshape(B * N, D), g2, tri, lw["expand"], lw["wgate"], lw["wup"], lw["wdown"])
    return out.reshape(B, N, D)


def _rope_partner():
    p = np.arange(SLOT)
    for j in range(ROPE_HALF):
        for base in (QK_NOPE, QK_NOPE + ROPE_AXIS):
            p[base + j] = base + ROPE_HALF + j
            p[base + ROPE_HALF + j] = base + j
    return p


def _rope_tables(n):
    rows = n // GRID_W
    row = jnp.repeat(jnp.arange(rows, dtype=F32), GRID_W)
    col = jnp.tile(jnp.arange(GRID_W, dtype=F32), rows)
    freqs = ROPE_BASE ** (-jnp.arange(ROPE_HALF, dtype=F32) / ROPE_HALF)
    ar, ac = row[:, None] * freqs, col[:, None] * freqs
    one = jnp.ones((n, QK_NOPE), F32)
    zero = jnp.zeros((n, QK_NOPE), F32)
    pad1 = jnp.ones((n, SLOT - QK_DIM), F32)
    pad0 = jnp.zeros((n, SLOT - QK_DIM), F32)
    cos_t = jnp.concatenate([one, jnp.cos(ar), jnp.cos(ar), jnp.cos(ac), jnp.cos(ac), pad1], axis=1)
    sin_t = jnp.concatenate([zero, -jnp.sin(ar), jnp.sin(ar), -jnp.sin(ac), jnp.sin(ac), pad0], axis=1)
    return cos_t, sin_t


def _slot_cols(w, width):
    kdim = w.shape[0]
    w = w.reshape(kdim, N_HEADS, width)
    return jnp.pad(w, ((0, 0), (0, 0), (0, SLOT - width))).reshape(kdim, HSLOTS)


def _layer_weights(l, p, mf_all):
    partner = _rope_partner()
    w_in = p["w_in"][l]
    kr_cols = w_in[:, Q_LORA + KV_LORA:Q_LORA + KV_LORA + QK_ROPE]
    kr_slot = jnp.pad(kr_cols, ((0, 0), (QK_NOPE, SLOT - QK_DIM)))
    kr_swap = jnp.where((np.arange(SLOT) >= QK_NOPE) & (np.arange(SLOT) < QK_DIM), kr_slot[:, partner], 0.0)
    win = jnp.concatenate([w_in[:, :Q_LORA + KV_LORA], kr_slot, kr_swap, w_in[:, Q_LORA + KV_LORA + QK_ROPE:]], axis=1)

    rope_lane = (np.arange(SLOT) >= QK_NOPE) & (np.arange(SLOT) < QK_DIM)
    wq = _slot_cols(p["w_q_b"][l], QK_DIM)
    wq3 = wq.reshape(Q_LORA, N_HEADS, SLOT)
    wqs = jnp.where(rope_lane, wq3[:, :, partner], 0.0).reshape(Q_LORA, HSLOTS)

    wkv = p["w_kv_b"][l].reshape(KV_LORA, N_HEADS, QK_NOPE + V_DIM)
    wk = jnp.pad(wkv[:, :, :QK_NOPE], ((0, 0), (0, 0), (0, SLOT - QK_NOPE))).reshape(KV_LORA, HSLOTS)
    wv = wkv[:, :, QK_NOPE:].reshape(KV_LORA, MLA_WIDTH)

    def gains(g, scale):
        gp = jnp.pad(g, (0, SLOT - QK_DIM))
        gs = jnp.where(rope_lane, gp[partner], 0.0)
        return (jnp.tile(gp, N_HEADS) * scale)[None, :], (jnp.tile(gs, N_HEADS) * scale)[None, :]

    gq, gqs = gains(p["q_norm"][l], QK_DIM ** -0.5 * math.log2(math.e))
    gk, gks = gains(p["k_norm"][l], 1.0)

    def group_cols(w):
        return w.reshape(N_EGROUPS, EPG, D_MODEL, D_EXPERT).transpose(0, 2, 1, 3).reshape(
            N_EGROUPS, D_MODEL, EPG * D_EXPERT).astype(BF16)

    return {
        "gmix": p["norm_mix"][l][None, :],
        "gffn": p["norm_ffn"][l][None, :],
        "win": win.astype(BF16),
        "qa": p["q_a_norm"][l][None, :],
        "wq": wq.astype(BF16),
        "wqs": wqs.astype(BF16),
        "kva": p["kv_a_norm"][l][None, :],
        "wk": wk.astype(BF16),
        "wv": wv.astype(BF16),
        "gq": gq, "gqs": gqs, "gk": gk, "gks": gks,
        "mf": mf_all[l],
        "wout": p["w_out"][l].astype(BF16),
        "wgate": group_cols(p["w_gate"][l]),
        "wup": group_cols(p["w_up"][l]),
        "wdown": p["w_down"][l].reshape(N_EGROUPS, EPG * D_EXPERT, D_MODEL).astype(BF16),
    }


def _const_tables():
    lane_head = np.arange(2 * SLOT) // SLOT
    esum = (lane_head[:, None] == lane_head[None, :]).astype(np.float32)
    ex = np.zeros((ROUTE_ROWS, EPG * D_EXPERT), np.float32)
    for i in range(EPG):
        ex[i, i * D_EXPERT:(i + 1) * D_EXPERT] = 1.0
        ex[EPG + i, i * D_EXPERT:(i + 1) * D_EXPERT] = 1.0
    return jnp.asarray(esum, BF16), jnp.asarray(ex, BF16)


def _mod6(mod_l, rows, B):
    m = mod_l[jnp.asarray(rows)]
    return [m[:, i * D_MODEL:(i + 1) * D_MODEL][:, None, :] for i in range(6)]


def kernel(x, c, ctx, c_ctx, w_ada, b_ada, norm_mix, norm_ffn, w_in, q_a_norm, w_q_b, kv_a_norm, w_kv_b, q_norm,
           k_norm, w_fourier, w_out, router_w, router_bias, w_gate, w_up, w_down):
    B, S, D = x.shape
    n_ctx = ctx.shape[1]
    depth = w_ada.shape[0]
    p = dict(w_in=w_in, q_a_norm=q_a_norm, w_q_b=w_q_b, kv_a_norm=kv_a_norm, w_kv_b=w_kv_b, q_norm=q_norm,
             k_norm=k_norm, norm_mix=norm_mix, norm_ffn=norm_ffn, w_out=w_out, w_gate=w_gate, w_up=w_up,
             w_down=w_down)

    c8 = jnp.concatenate([c, c_ctx[None, :], jnp.zeros((8 - B - 1, D), F32)], axis=0)
    mod = _ada_call(c8, w_ada, b_ada)
    mf_all = _fmat_call(w_fourier)
    esum, expand = _const_tables()
    cos_l, sin_l = _rope_tables(S)
    cos_c = jnp.ones((n_ctx, SLOT), F32)
    sin_c = jnp.zeros((n_ctx, SLOT), F32)
    rwt = router_w.T
    rb = router_bias[:, None]

    tm_lat = 512
    tm_ctx = min(256, n_ctx)
    for l in range(depth):
        last = l == depth - 1
        lw = _layer_weights(l, p, mf_all)
        lw["esum"] = esum
        lw["expand"] = expand
        sh1, sc1, g1, sh2, sc2, g2 = _mod6(mod[l], list(range(B)), B)
        csh1, csc1, cg1, csh2, csc2, cg2 = _mod6(mod[l], [B] * B, B)

        q_c, k_c, v_c, wri_c = _premix_call(ctx, csh1, csc1, lw, cos_c, sin_c, tm_ctx)
        q_l, k_l, v_l, wri_l = _premix_call(x, sh1, sc1, lw, cos_l, sin_l, tm_lat)
        k_all = jnp.concatenate([k_c, k_l], axis=1)
        v_all = jnp.concatenate([v_c, v_l], axis=1)
        bounded = _scores_bounded(q_norm[l], k_norm[l])
        o_l = _attention(q_l, k_all, v_all, 512, bounded)
        z_l = _fft_call(wri_l)
        x_mid, f_l, wd_l = _postmix_call(o_l, z_l, x, g1, sh2, sc2, lw, rwt, rb, tm_lat)
        x = _moe_call(f_l, wd_l, x_mid, g2, lw, tm_lat)
        if not last:
            o_c = _attention(q_c, k_c, v_c, tm_ctx, bounded)
            z_c = _dft_small_call(wri_c)
            c_mid, f_c, wd_c = _postmix_call(o_c, z_c, ctx, cg1, csh2, csc2, lw, rwt, rb, tm_ctx)
            ctx = _moe_call(f_c, wd_c, c_mid, cg2, lw, tm_ctx)
    return x
```

```python
import functools
import math

import numpy as np
import jax
import jax.numpy as jnp
from jax import lax
from jax.experimental import pallas as pl
from jax.experimental.pallas import tpu as pltpu

F32 = jnp.float32
BF16 = jnp.bfloat16
HIGHEST = lax.Precision.HIGHEST

D_MODEL = 1024
N_HEADS = 8
QK_NOPE = 64
QK_ROPE = 32
QK_DIM = QK_NOPE + QK_ROPE
V_DIM = 64
Q_LORA = 256
KV_LORA = 128
ROPE_AXIS = QK_ROPE // 2
ROPE_HALF = ROPE_AXIS // 2
ROPE_BASE = 10000.0
GRID_W = 64
F_GROUPS = 4
F_GDIM = 128
F_WIDTH = F_GROUPS * F_GDIM
MLA_WIDTH = N_HEADS * V_DIM
N_EXPERTS = 16
N_EGROUPS = 4
EPG = N_EXPERTS // N_EGROUPS
D_EXPERT = 256
EPS = 1e-6
ROUTE_ROWS = 16

LANES = 128
SLOT = LANES
HSLOTS = N_HEADS * SLOT
VMEM_LIMIT = 56 * 1024 * 1024

C_CQ = 0
C_CKV = C_CQ + Q_LORA
C_KR = C_CKV + KV_LORA
C_KRS = C_KR + SLOT
C_F = C_KRS + SLOT
IN_COLS = C_F + F_WIDTH


def _cparams(sem):
    return pltpu.CompilerParams(dimension_semantics=sem, vmem_limit_bytes=VMEM_LIMIT)


def _full(shape):
    n = len(shape)
    return pl.BlockSpec(shape, lambda *_: (0,) * n)


def _ada_kernel(c_ref, w_ref, b_ref, o_ref):
    c = c_ref[...]
    s = c * jax.nn.sigmoid(c)
    o_ref[0] = jnp.dot(s, w_ref[0], preferred_element_type=F32, precision=HIGHEST) + b_ref[0]


def _ada_call(c8, w_ada, b_ada):
    L, D, W = w_ada.shape
    tn = 1536
    return pl.pallas_call(
        _ada_kernel,
        grid=(L, W // tn),
        in_specs=[
            pl.BlockSpec((8, D), lambda l, j: (0, 0)),
            pl.BlockSpec((1, D, tn), lambda l, j: (l, 0, j)),
            pl.BlockSpec((1, 1, tn), lambda l, j: (l, 0, j)),
        ],
        out_specs=pl.BlockSpec((1, 8, tn), lambda l, j: (l, 0, j)),
        out_shape=jax.ShapeDtypeStruct((L, 8, W), F32),
        compiler_params=_cparams(("parallel", "parallel")),
        name="adaln_mod",
    )(c8, w_ada, b_ada.reshape(L, 1, W))


def _fmat_kernel(cc_ref, sc_ref, wf_ref, o_ref):
    o_ref[...] = jnp.zeros(o_ref.shape, o_ref.dtype)
    for g in range(F_GROUPS):
        wf = wf_ref[0, g]
        mr = jnp.dot(cc_ref[...], wf, preferred_element_type=F32, precision=HIGHEST)
        mi = jnp.dot(sc_ref[...], wf, preferred_element_type=F32, precision=HIGHEST)
        r0 = g * F_GDIM
        o_ref[0, r0:r0 + F_GDIM, r0:r0 + F_GDIM] = mr.astype(o_ref.dtype)
        o_ref[0, r0:r0 + F_GDIM, F_WIDTH + r0:F_WIDTH + r0 + F_GDIM] = mi.astype(o_ref.dtype)


def _fmat_call(w_fourier):
    L = w_fourier.shape[0]
    c = np.arange(F_GDIM)
    ang = 2.0 * np.pi * np.outer(c, c) / F_GDIM
    cc = jnp.asarray((np.cos(ang) / math.sqrt(F_GDIM)).astype(np.float32))
    sc = jnp.asarray((-np.sin(ang) / math.sqrt(F_GDIM)).astype(np.float32))
    return pl.pallas_call(
        _fmat_kernel,
        grid=(L,),
        in_specs=[
            _full((F_GDIM, F_GDIM)),
            _full((F_GDIM, F_GDIM)),
            pl.BlockSpec((1, F_GROUPS, F_GDIM, F_GDIM), lambda l: (l, 0, 0, 0)),
        ],
        out_specs=pl.BlockSpec((1, F_WIDTH, 2 * F_WIDTH), lambda l: (l, 0, 0)),
        out_shape=jax.ShapeDtypeStruct((L, F_WIDTH, 2 * F_WIDTH), BF16),
        compiler_params=_cparams(("parallel",)),
        name="fourier_chan_mats",
    )(cc, sc, w_fourier)


def _rms(x):
    return x * lax.rsqrt(jnp.mean(x * x, axis=-1, keepdims=True) + EPS)


def _tile8(a):
    return jnp.concatenate([a] * N_HEADS, axis=-1)


def _head_sums(x2, e2):
    w = e2.shape[0]
    xb = x2.astype(BF16)
    return jnp.concatenate(
        [jnp.dot(xb[:, c * w:(c + 1) * w], e2, preferred_element_type=F32) for c in range(HSLOTS // w)], axis=-1)


def _premix_kernel(x_ref, sh_ref, sc_ref, gn_ref, win_ref, qa_ref, wq_ref, wqs_ref, kva_ref, wk_ref, wv_ref,
                   e_ref, cos_ref, sin_ref, gq_ref, gqs_ref, gk_ref, gks_ref, mf_ref,
                   q_ref, k_ref, v_ref, wri_ref):
    x = x_ref[0]
    h = _rms(x) * gn_ref[...] * (1.0 + sc_ref[0]) + sh_ref[0]
    proj = jnp.dot(h.astype(BF16), win_ref[...], preferred_element_type=F32)
    cq = proj[:, C_CQ:C_CKV]
    ckv = proj[:, C_CKV:C_KR]
    kr = proj[:, C_KR:C_KRS]
    krs = proj[:, C_KRS:C_F]
    f = proj[:, C_F:IN_COLS]

    cos8 = _tile8(cos_ref[...])
    sin8 = _tile8(sin_ref[...])
    inv_d = 1.0 / QK_DIM

    cqn = (_rms(cq) * qa_ref[...]).astype(BF16)
    qr = jnp.dot(cqn, wq_ref[...], preferred_element_type=F32)
    qs = jnp.dot(cqn, wqs_ref[...], preferred_element_type=F32)
    ssq = _head_sums(qr * qr, e_ref[...])
    rq = lax.rsqrt(ssq * inv_d + EPS)
    q = rq * (qr * (gq_ref[...] * cos8) + qs * (gqs_ref[...] * sin8))
    q_ref[0] = q.astype(q_ref.dtype)

    ckvn = (_rms(ckv) * kva_ref[...]).astype(BF16)
    kraw = jnp.dot(ckvn, wk_ref[...], preferred_element_type=F32) + _tile8(kr)
    ssk = _head_sums(kraw * kraw, e_ref[...])
    rk = lax.rsqrt(ssk * inv_d + EPS)
    k = rk * (kraw * (gk_ref[...] * cos8) + _tile8(krs) * (gks_ref[...] * sin8))
    k_ref[0] = k.astype(k_ref.dtype)
    v_ref[0] = jnp.dot(ckvn, wv_ref[...], preferred_element_type=F32).astype(v_ref.dtype)

    w = jnp.dot(f.astype(BF16), mf_ref[...], preferred_element_type=F32)
    wri_ref[0, 0] = w[:, :F_WIDTH].astype(wri_ref.dtype)
    wri_ref[0, 1] = w[:, F_WIDTH:].astype(wri_ref.dtype)


def _premix_call(x, sh, sc, lw, cos_t, sin_t, tm):
    B, N, D = x.shape
    row = lambda b, i: (b, i, 0)
    vec = lambda b, i: (b, 0, 0)
    in_specs = [
        pl.BlockSpec((1, tm, D), row),
        pl.BlockSpec((1, 1, D), vec),
        pl.BlockSpec((1, 1, D), vec),
        _full((1, D)),
        _full((D, IN_COLS)),
        _full((1, Q_LORA)),
        _full((Q_LORA, HSLOTS)),
        _full((Q_LORA, HSLOTS)),
        _full((1, KV_LORA)),
        _full((KV_LORA, HSLOTS)),
        _full((KV_LORA, MLA_WIDTH)),
        _full((2 * SLOT, 2 * SLOT)),
        pl.BlockSpec((tm, SLOT), lambda b, i: (i, 0)),
        pl.BlockSpec((tm, SLOT), lambda b, i: (i, 0)),
        _full((1, HSLOTS)),
        _full((1, HSLOTS)),
        _full((1, HSLOTS)),
        _full((1, HSLOTS)),
        _full((F_WIDTH, 2 * F_WIDTH)),
    ]
    out_specs = [
        pl.BlockSpec((1, tm, HSLOTS), row),
        pl.BlockSpec((1, tm, HSLOTS), row),
        pl.BlockSpec((1, tm, MLA_WIDTH), row),
        pl.BlockSpec((1, 2, tm, F_WIDTH), lambda b, i: (b, 0, i, 0)),
    ]
    out_shape = [
        jax.ShapeDtypeStruct((B, N, HSLOTS), BF16),
        jax.ShapeDtypeStruct((B, N, HSLOTS), BF16),
        jax.ShapeDtypeStruct((B, N, MLA_WIDTH), BF16),
        jax.ShapeDtypeStruct((B, 2, N, F_WIDTH), BF16),
    ]
    return pl.pallas_call(
        _premix_kernel,
        grid=(B, N // tm),
        in_specs=in_specs,
        out_specs=out_specs,
        out_shape=out_shape,
        compiler_params=_cparams(("parallel", "parallel")),
        name="premix_proj",
    )(x, sh, sc, lw["gmix"], lw["win"], lw["qa"], lw["wq"], lw["wqs"], lw["kva"], lw["wk"], lw["wv"],
      lw["esum"], cos_t, sin_t, lw["gq"], lw["gqs"], lw["gk"], lw["gks"], lw["mf"])


def _attn_kernel(q_ref, k_ref, v_ref, ot_ref, vt_scr, qt_scr, *, n_kt, bounded):
    tq = q_ref.shape[1]
    tk = ATTN_TK

    @pl.when(pl.program_id(2) == 0)
    def _():
        odd = (pl.program_id(1) % 2) == 1
        ones = jnp.ones((VT_ROWS - V_DIM, tk), BF16)
        for t in range(n_kt):
            vt_pair = v_ref[0, t * tk:(t + 1) * tk, :].T
            vt_scr[t, :V_DIM, :] = jnp.where(odd, vt_pair[V_DIM:], vt_pair[:V_DIM])
            vt_scr[t, V_DIM:, :] = ones

    qt_scr[...] = q_ref[0].T
    qt = qt_scr[...]

    def scores(t):
        off = t * tk if isinstance(t, int) else pl.multiple_of(t * tk, tk)
        return jnp.dot(k_ref[0, pl.ds(off, tk), :], qt, preferred_element_type=F32)

    def pv(t, p):
        return jnp.dot(vt_scr[t], p.astype(BF16), preferred_element_type=F32)

    if bounded:
        acc = jnp.zeros((VT_ROWS, tq), F32)
        s = scores(0)
        for t in range(n_kt):
            s_next = scores(t + 1) if t + 1 < n_kt else None
            acc = acc + pv(t, jnp.exp2(s))
            s = s_next
    else:
        unroll = _attn_unroll(n_kt)

        def body(g, carry):
            m, acc, s = carry
            for u in range(unroll):
                t = g * unroll + u
                s_next = scores(jnp.minimum(t + 1, n_kt - 1))
                m_new = jnp.maximum(m, jnp.max(s, axis=0, keepdims=True))
                acc = jnp.exp2(m - m_new) * acc + pv(t, jnp.exp2(s - m_new))
                m, s = m_new, s_next
            return m, acc, s

        init = (jnp.full((1, tq), -1e30, F32), jnp.zeros((VT_ROWS, tq), F32), scores(0))
        _, acc, _ = lax.fori_loop(0, n_kt // unroll, body, init)
    ot_ref[0] = (acc[:V_DIM] / acc[V_DIM:V_DIM + 1]).astype(ot_ref.dtype)


ATTN_TK = 256
VT_ROWS = V_DIM + 16
ATTN_SCORE_BOUND = 60.0


def _attn_unroll(n_kt):
    for u in (13, 9, 8, 5, 4, 3, 2):
        if n_kt % u == 0:
            return u
    return 1


def _attn_call(q, k, v, tq, bounded):
    B, Nq, _ = q.shape
    Nk = k.shape[1]
    n_kt = Nk // ATTN_TK
    kern = functools.partial(_attn_kernel, n_kt=n_kt, bounded=bounded)
    return pl.pallas_call(
        kern,
        grid=(B, N_HEADS, Nq // tq),
        in_specs=[
            pl.BlockSpec((1, tq, SLOT), lambda b, h, i: (b, i, h)),
            pl.BlockSpec((1, Nk, SLOT), lambda b, h, i: (b, 0, h)),
            pl.BlockSpec((1, Nk, 2 * V_DIM), lambda b, h, i: (b, 0, h // 2)),
        ],
        out_specs=pl.BlockSpec((1, V_DIM, tq), lambda b, h, i: (b, h, i)),
        out_shape=jax.ShapeDtypeStruct((B, MLA_WIDTH, Nq), BF16),
        scratch_shapes=[pltpu.VMEM((n_kt, VT_ROWS, ATTN_TK), BF16), pltpu.VMEM((SLOT, tq), BF16)],
        compiler_params=_cparams(("arbitrary", "arbitrary", "arbitrary")),
        name="mla_attention_bounded" if bounded else "mla_attention_online",
    )(q, k, v)


def _attention(q, k, v, tq, scores_bounded):
    return lax.cond(scores_bounded,
                    lambda a, b, c: _attn_call(a, b, c, tq, True),
                    lambda a, b, c: _attn_call(a, b, c, tq, False),
                    q, k, v)


def _scores_bounded(q_gain, k_gain):
    bound = math.sqrt(QK_DIM) * math.log2(math.e) * jnp.max(jnp.abs(q_gain)) * jnp.max(jnp.abs(k_gain))
    return bound <= ATTN_SCORE_BOUND


def _fft1_kernel(f1_ref, tr_ref, ti_ref, w_ref, y_ref, *, n1, n2_per):
    y = jnp.dot(f1_ref[...], w_ref[0], preferred_element_type=F32)
    yr = y[:n1]
    yi = y[n1:]
    for t in range(n2_per):
        tr = tr_ref[:, t * LANES:(t + 1) * LANES]
        ti = ti_ref[:, t * LANES:(t + 1) * LANES]
        tr4 = jnp.concatenate([tr] * F_GROUPS, axis=-1)
        ti4 = jnp.concatenate([ti] * F_GROUPS, axis=-1)
        a = yr[:, t * F_WIDTH:(t + 1) * F_WIDTH]
        b = yi[:, t * F_WIDTH:(t + 1) * F_WIDTH]
        y_ref[0, :n1, t * F_WIDTH:(t + 1) * F_WIDTH] = (a * tr4 - b * ti4).astype(y_ref.dtype)
        y_ref[0, n1:, t * F_WIDTH:(t + 1) * F_WIDTH] = (a * ti4 + b * tr4).astype(y_ref.dtype)


def _fft2_kernel(c2_ref, s2_ref, y_ref, o_ref, *, kc):
    for i in range(kc):
        re = jnp.dot(c2_ref[...], y_ref[0, 0, i], preferred_element_type=F32)
        im = jnp.dot(s2_ref[...], y_ref[0, 1, i], preferred_element_type=F32)
        o_ref[0, :, i * F_WIDTH:(i + 1) * F_WIDTH] = (re + im).astype(o_ref.dtype)


def _fft_consts(n):
    n2 = LANES
    n1 = n // n2
    a1 = 2.0 * np.pi * np.outer(np.arange(n1), np.arange(n1)) / n1
    c1, s1 = np.cos(a1), np.sin(a1)
    f1 = np.block([[c1, s1], [-s1, c1]]).astype(np.float32)
    at = 2.0 * np.pi * np.outer(np.arange(n1), np.arange(n2)) / n
    tr = np.repeat(np.cos(at), LANES, axis=1).astype(np.float32)
    ti = np.repeat(-np.sin(at), LANES, axis=1).astype(np.float32)
    a2 = 2.0 * np.pi * np.outer(np.arange(n2), np.arange(n2)) / n2
    scale = 1.0 / math.sqrt(n)
    c2 = (np.cos(a2) * scale).astype(np.float32)
    s2 = (np.sin(a2) * scale).astype(np.float32)
    return n1, n2, f1, tr, ti, c2, s2


def _fft_call(wri):
    B, _, N, _ = wri.shape
    n1, n2, f1, tr, ti, c2, s2 = _fft_consts(N)
    cols = n2 * F_WIDTH
    n2_per = 4
    tc = n2_per * F_WIDTH
    y = pl.pallas_call(
        functools.partial(_fft1_kernel, n1=n1, n2_per=n2_per),
        grid=(B, cols // tc),
        in_specs=[
            _full((2 * n1, 2 * n1)),
            pl.BlockSpec((n1, n2_per * LANES), lambda b, j: (0, j)),
            pl.BlockSpec((n1, n2_per * LANES), lambda b, j: (0, j)),
            pl.BlockSpec((1, 2 * n1, tc), lambda b, j: (b, 0, j)),
        ],
        out_specs=pl.BlockSpec((1, 2 * n1, tc), lambda b, j: (b, 0, j)),
        out_shape=jax.ShapeDtypeStruct((B, 2 * n1, cols), BF16),
        compiler_params=_cparams(("parallel", "parallel")),
        name="fft_stage1",
    )(jnp.asarray(f1, BF16), jnp.asarray(tr), jnp.asarray(ti), wri.reshape(B, 2 * n1, cols))
    kc = min(8, n1)
    out = pl.pallas_call(
        functools.partial(_fft2_kernel, kc=kc),
        grid=(B, n1 // kc),
        in_specs=[
            _full((n2, n2)),
            _full((n2, n2)),
            pl.BlockSpec((1, 2, kc, n2, F_WIDTH), lambda b, j: (b, 0, j, 0, 0)),
        ],
        out_specs=pl.BlockSpec((1, n2, kc * F_WIDTH), lambda b, j: (b, 0, j)),
        out_shape=jax.ShapeDtypeStruct((B, n2, n1 * F_WIDTH), BF16),
        compiler_params=_cparams(("parallel", "parallel")),
        name="fft_stage2",
    )(jnp.asarray(c2, BF16), jnp.asarray(s2, BF16), y.reshape(B, 2, n1, n2, F_WIDTH))
    return out.reshape(B, N, F_WIDTH)


def _dft_small_kernel(c_ref, s_ref, w_ref, o_ref):
    re = jnp.dot(c_ref[...], w_ref[0, 0], preferred_element_type=F32)
    im = jnp.dot(s_ref[...], w_ref[0, 1], preferred_element_type=F32)
    o_ref[0] = (re + im).astype(o_ref.dtype)


def _dft_small_call(wri):
    B, _, N, _ = wri.shape
    a = 2.0 * np.pi * np.outer(np.arange(N), np.arange(N)) / N
    c = (np.cos(a) / math.sqrt(N)).astype(np.float32)
    s = (np.sin(a) / math.sqrt(N)).astype(np.float32)
    return pl.pallas_call(
        _dft_small_kernel,
        grid=(B,),
        in_specs=[_full((N, N)), _full((N, N)), pl.BlockSpec((1, 2, N, F_WIDTH), lambda b: (b, 0, 0, 0))],
        out_specs=pl.BlockSpec((1, N, F_WIDTH), lambda b: (b, 0, 0)),
        out_shape=jax.ShapeDtypeStruct((B, N, F_WIDTH), BF16),
        compiler_params=_cparams(("parallel",)),
        name="dft_small",
    )(jnp.asarray(c, BF16), jnp.asarray(s, BF16), wri)


def _router_rows(logits, bias):
    t = logits.shape[1]
    s = jax.nn.sigmoid(logits)
    b = s + bias
    srow = [s[e:e + 1, :] for e in range(N_EXPERTS)]
    brow = [b[e:e + 1, :] for e in range(N_EXPERTS)]
    gscore = []
    for g in range(N_EGROUPS):
        b0, b1, b2, b3 = brow[EPG * g:EPG * g + EPG]
        m1, n1 = jnp.maximum(b0, b1), jnp.minimum(b0, b1)
        m2, n2 = jnp.maximum(b2, b3), jnp.minimum(b2, b3)
        top1 = jnp.maximum(m1, m2)
        top2 = jnp.maximum(jnp.minimum(m1, m2), jnp.maximum(n1, n2))
        gscore.append(top1 + top2)
    best = gscore[0]
    gi = jnp.zeros((1, t), jnp.int32)
    for g in range(1, N_EGROUPS):
        take = gscore[g] > best
        gi = jnp.where(take, g, gi)
        best = jnp.where(take, gscore[g], best)
    ib, isc = [], []
    for i in range(EPG):
        vb, vs = brow[i], srow[i]
        for g in range(1, N_EGROUPS):
            sel = gi == g
            vb = jnp.where(sel, brow[EPG * g + i], vb)
            vs = jnp.where(sel, srow[EPG * g + i], vs)
        ib.append(vb)
        isc.append(vs)
    w = []
    for i in range(EPG):
        rank = jnp.zeros((1, t), jnp.int32)
        for j in range(EPG):
            if j == i:
                continue
            ahead = (ib[j] > ib[i]) | ((ib[j] == ib[i]) & (j < i))
            rank = rank + ahead.astype(jnp.int32)
        w.append(jnp.where(rank < 2, isc[i], 0.0))
    tot = (w[0] + w[1]) + (w[2] + w[3])
    inv = 1.0 / tot
    ridx = lax.broadcasted_iota(jnp.int32, (ROUTE_ROWS, t), 0)
    out = jnp.where(ridx == 2 * EPG, gi.astype(F32), 0.0)
    for i in range(EPG):
        out = jnp.where((ridx == i) | (ridx == EPG + i), w[i] * inv, out)
    return out


def _postmix_kernel(ot_ref, z_ref, x_ref, g1_ref, sh_ref, sc_ref, gn_ref, wo_ref, rw_ref, rb_ref,
                    xo_ref, f_ref, wd_ref):
    mix = lax.dot_general(ot_ref[0], wo_ref[:MLA_WIDTH, :], (((0,), (0,)), ((), ())), preferred_element_type=F32)
    mix = mix + jnp.dot(z_ref[0], wo_ref[MLA_WIDTH:, :], preferred_element_type=F32)
    xn = x_ref[0] + g1_ref[0] * mix
    xo_ref[0] = xn
    f = _rms(xn) * gn_ref[...] * (1.0 + sc_ref[0]) + sh_ref[0]
    f_ref[0] = f.astype(f_ref.dtype)
    logits = lax.dot_general(rw_ref[...], f, (((1,), (1,)), ((), ())), preferred_element_type=F32,
                             precision=HIGHEST)
    wd_ref[0] = _router_rows(logits, rb_ref[...])


def _postmix_call(o, z, x, g1, sh, sc, lw, rwt, rb, tm):
    B, N, D = x.shape
    row = lambda b, i: (b, i, 0)
    vec = lambda b, i: (b, 0, 0)
    return pl.pallas_call(
        _postmix_kernel,
        grid=(B, N // tm),
        in_specs=[
            pl.BlockSpec((1, MLA_WIDTH, tm), lambda b, i: (b, 0, i)),
            pl.BlockSpec((1, tm, F_WIDTH), row),
            pl.BlockSpec((1, tm, D), row),
            pl.BlockSpec((1, 1, D), vec),
            pl.BlockSpec((1, 1, D), vec),
            pl.BlockSpec((1, 1, D), vec),
            _full((1, D)),
            _full((D, D)),
            _full((N_EXPERTS, D)),
            _full((N_EXPERTS, 1)),
        ],
        out_specs=[
            pl.BlockSpec((1, tm, D), row),
            pl.BlockSpec((1, tm, D), row),
            pl.BlockSpec((1, ROUTE_ROWS, tm), lambda b, i: (b, 0, i)),
        ],
        out_shape=[
            jax.ShapeDtypeStruct((B, N, D), F32),
            jax.ShapeDtypeStruct((B, N, D), BF16),
            jax.ShapeDtypeStruct((B, ROUTE_ROWS, N), F32),
        ],
        compiler_params=_cparams(("parallel", "parallel")),
        name="postmix_router",
    )(o, z, x, g1, sh, sc, lw["gffn"], lw["wout"], rwt, rb)


def _expert_mlp(xb, wexp, g, wgate_ref, wup_ref, wdown_ref):
    hg = jnp.dot(xb, wgate_ref[g], preferred_element_type=F32)
    hu = jnp.dot(xb, wup_ref[g], preferred_element_type=F32)
    h = (hg * jax.nn.sigmoid(hg)) * hu * wexp
    return jnp.dot(h.astype(BF16), wdown_ref[g], preferred_element_type=F32)


def _moe_kernel(nch_ref, f_ref, rt_ref, x_ref, g2_ref, tri_ref, ex_ref, wgate_ref, wup_ref, wdown_ref, xo_ref, y_scr):
    tm = f_ref.shape[0]
    tile = pl.program_id(0)
    f = f_ref[...]
    rt = rt_ref[0]
    gid = rt[2 * EPG:2 * EPG + 1, :]
    r16 = lax.broadcasted_iota(jnp.int32, (ROUTE_ROWS, tm), 0)
    hi = rt.astype(BF16).astype(F32)
    whl = jnp.where(r16 < EPG, hi, jnp.where(r16 < 2 * EPG, rt - hi, 0.0)).astype(BF16)
    onehot = (r16.astype(F32) == gid).astype(BF16)
    before = jnp.dot(onehot, tri_ref[...], preferred_element_type=F32)
    rank = jnp.sum(onehot.astype(F32) * before, axis=0, keepdims=True)
    slot0 = lax.broadcasted_iota(jnp.int32, (MOE_CHUNK, tm), 0)
    y_scr[...] = jnp.zeros(y_scr.shape, y_scr.dtype)
    for g in range(N_EGROUPS):
        rank_g = jnp.where(gid == g, rank, -1.0)

        def chunk(c, carry, g=g, rank_g=rank_g):
            sel = ((slot0 + c * MOE_CHUNK).astype(F32) == rank_g).astype(BF16)
            xg = jnp.dot(sel, f, preferred_element_type=F32).astype(BF16)
            wr = lax.dot_general(sel, whl, (((1,), (1,)), ((), ())), preferred_element_type=F32).astype(BF16)
            wexp = jnp.dot(wr, ex_ref[...], preferred_element_type=F32)
            yg = _expert_mlp(xg, wexp, g, wgate_ref, wup_ref, wdown_ref)
            y_scr[...] += lax.dot_general(sel, yg.astype(BF16), (((0,), (0,)), ((), ())),
                                          preferred_element_type=F32)
            return carry

        lax.fori_loop(0, nch_ref[tile * N_EGROUPS + g], chunk, 0)
    xo_ref[...] = x_ref[...] + g2_ref[0] * y_scr[...]


MOE_CHUNK = 128


def _moe_call(f, rt, x, g2, lw, tm):
    B, N, D = x.shape
    GW = EPG * D_EXPERT
    tpb = N // tm
    n_tiles = B * tpb
    gid = rt[:, 2 * EPG, :].reshape(n_tiles, tm)
    counts = jnp.stack([jnp.sum(gid == g, axis=1) for g in range(N_EGROUPS)], axis=1)
    nch = ((counts + MOE_CHUNK - 1) // MOE_CHUNK).astype(jnp.int32).reshape(n_tiles * N_EGROUPS)
    tri = jnp.asarray(np.triu(np.ones((tm, tm), np.float32), 1), BF16)
    resident = dict(pipeline_mode=pl.Buffered(1))
    grid_spec = pltpu.PrefetchScalarGridSpec(
        num_scalar_prefetch=1,
        grid=(n_tiles,),
        in_specs=[
            pl.BlockSpec((tm, D), lambda i, o: (i, 0)),
            pl.BlockSpec((1, ROUTE_ROWS, tm), lambda i, o: (i // tpb, 0, i % tpb)),
            pl.BlockSpec((tm, D), lambda i, o: (i, 0)),
            pl.BlockSpec((1, 1, D), lambda i, o: (i // tpb, 0, 0)),
            pl.BlockSpec((tm, tm), lambda i, o: (0, 0), **resident),
            pl.BlockSpec((ROUTE_ROWS, GW), lambda i, o: (0, 0), **resident),
            pl.BlockSpec((N_EGROUPS, D, GW), lambda i, o: (0, 0, 0), **resident),
            pl.BlockSpec((N_EGROUPS, D, GW), lambda i, o: (0, 0, 0), **resident),
            pl.BlockSpec((N_EGROUPS, GW, D), lambda i, o: (0, 0, 0), **resident),
        ],
        out_specs=pl.BlockSpec((tm, D), lambda i, o: (i, 0)),
        scratch_shapes=[pltpu.VMEM((tm, D), F32)],
    )
    out = pl.pallas_call(
        _moe_kernel,
        grid_spec=grid_spec,
        out_shape=jax.ShapeDtypeStruct((B * N, D), F32),
        compiler_params=_cparams(("arbitrary",)),
        name="moe_experts",
    )(nch, f.reshape(B * N, D), rt, x.reshape(B * N, D), g2, tri, lw["expand"], lw["wgate"], lw["wup"], lw["wdown"])
    return out.reshape(B, N, D)


def _rope_partner():
    p = np.arange(SLOT)
    for j in range(ROPE_HALF):
        for base in (QK_NOPE, QK_NOPE + ROPE_AXIS):
            p[base + j] = base + ROPE_HALF + j
            p[base + ROPE_HALF + j] = base + j
    return p


def _rope_tables(n):
    rows = n // GRID_W
    row = jnp.repeat(jnp.arange(rows, dtype=F32), GRID_W)
    col = jnp.tile(jnp.arange(GRID_W, dtype=F32), rows)
    freqs = ROPE_BASE ** (-jnp.arange(ROPE_HALF, dtype=F32) / ROPE_HALF)
    ar, ac = row[:, None] * freqs, col[:, None] * freqs
    one = jnp.ones((n, QK_NOPE), F32)
    zero = jnp.zeros((n, QK_NOPE), F32)
    pad1 = jnp.ones((n, SLOT - QK_DIM), F32)
    pad0 = jnp.zeros((n, SLOT - QK_DIM), F32)
    cos_t = jnp.concatenate([one, jnp.cos(ar), jnp.cos(ar), jnp.cos(ac), jnp.cos(ac), pad1], axis=1)
    sin_t = jnp.concatenate([zero, -jnp.sin(ar), jnp.sin(ar), -jnp.sin(ac), jnp.sin(ac), pad0], axis=1)
    return cos_t, sin_t


def _slot_cols(w, width):
    kdim = w.shape[0]
    w = w.reshape(kdim, N_HEADS, width)
    return jnp.pad(w, ((0, 0), (0, 0), (0, SLOT - width))).reshape(kdim, HSLOTS)


def _layer_weights(l, p, mf_all):
    partner = _rope_partner()
    w_in = p["w_in"][l]
    kr_cols = w_in[:, Q_LORA + KV_LORA:Q_LORA + KV_LORA + QK_ROPE]
    kr_slot = jnp.pad(kr_cols, ((0, 0), (QK_NOPE, SLOT - QK_DIM)))
    kr_swap = jnp.where((np.arange(SLOT) >= QK_NOPE) & (np.arange(SLOT) < QK_DIM), kr_slot[:, partner], 0.0)
    win = jnp.concatenate([w_in[:, :Q_LORA + KV_LORA], kr_slot, kr_swap, w_in[:, Q_LORA + KV_LORA + QK_ROPE:]], axis=1)

    rope_lane = (np.arange(SLOT) >= QK_NOPE) & (np.arange(SLOT) < QK_DIM)
    wq = _slot_cols(p["w_q_b"][l], QK_DIM)
    wq3 = wq.reshape(Q_LORA, N_HEADS, SLOT)
    wqs = jnp.where(rope_lane, wq3[:, :, partner], 0.0).reshape(Q_LORA, HSLOTS)

    wkv = p["w_kv_b"][l].reshape(KV_LORA, N_HEADS, QK_NOPE + V_DIM)
    wk = jnp.pad(wkv[:, :, :QK_NOPE], ((0, 0), (0, 0), (0, SLOT - QK_NOPE))).reshape(KV_LORA, HSLOTS)
    wv = wkv[:, :, QK_NOPE:].reshape(KV_LORA, MLA_WIDTH)

    def gains(g, scale):
        gp = jnp.pad(g, (0, SLOT - QK_DIM))
        gs = jnp.where(rope_lane, gp[partner], 0.0)
        return (jnp.tile(gp, N_HEADS) * scale)[None, :], (jnp.tile(gs, N_HEADS) * scale)[None, :]

    gq, gqs = gains(p["q_norm"][l], QK_DIM ** -0.5 * math.log2(math.e))
    gk, gks = gains(p["k_norm"][l], 1.0)

    def group_cols(w):
        return w.reshape(N_EGROUPS, EPG, D_MODEL, D_EXPERT).transpose(0, 2, 1, 3).reshape(
            N_EGROUPS, D_MODEL, EPG * D_EXPERT).astype(BF16)

    return {
        "gmix": p["norm_mix"][l][None, :],
        "gffn": p["norm_ffn"][l][None, :],
        "win": win.astype(BF16),
        "qa": p["q_a_norm"][l][None, :],
        "wq": wq.astype(BF16),
        "wqs": wqs.astype(BF16),
        "kva": p["kv_a_norm"][l][None, :],
        "wk": wk.astype(BF16),
        "wv": wv.astype(BF16),
        "gq": gq, "gqs": gqs, "gk": gk, "gks": gks,
        "mf": mf_all[l],
        "wout": p["w_out"][l].astype(BF16),
        "wgate": group_cols(p["w_gate"][l]),
        "wup": group_cols(p["w_up"][l]),
        "wdown": p["w_down"][l].reshape(N_EGROUPS, EPG * D_EXPERT, D_MODEL).astype(BF16),
    }


def _const_tables():
    lane_head = np.arange(2 * SLOT) // SLOT
    esum = (lane_head[:, None] == lane_head[None, :]).astype(np.float32)
    ex = np.zeros((ROUTE_ROWS, EPG * D_EXPERT), np.float32)
    for i in range(EPG):
        ex[i, i * D_EXPERT:(i + 1) * D_EXPERT] = 1.0
        ex[EPG + i, i * D_EXPERT:(i + 1) * D_EXPERT] = 1.0
    return jnp.asarray(esum, BF16), jnp.asarray(ex, BF16)


def _mod6(mod_l, rows, B):
    m = mod_l[jnp.asarray(rows)]
    return [m[:, i * D_MODEL:(i + 1) * D_MODEL][:, None, :] for i in range(6)]


def kernel(x, c, ctx, c_ctx, w_ada, b_ada, norm_mix, norm_ffn, w_in, q_a_norm, w_q_b, kv_a_norm, w_kv_b, q_norm,
           k_norm, w_fourier, w_out, router_w, router_bias, w_gate, w_up, w_down):
    B, S, D = x.shape
    n_ctx = ctx.shape[1]
    depth = w_ada.shape[0]
    p = dict(w_in=w_in, q_a_norm=q_a_norm, w_q_b=w_q_b, kv_a_norm=kv_a_norm, w_kv_b=w_kv_b, q_norm=q_norm,
             k_norm=k_norm, norm_mix=norm_mix, norm_ffn=norm_ffn, w_out=w_out, w_gate=w_gate, w_up=w_up,
             w_down=w_down)

    c8 = jnp.concatenate([c, c_ctx[None, :], jnp.zeros((8 - B - 1, D), F32)], axis=0)
    mod = _ada_call(c8, w_ada, b_ada)
    mf_all = _fmat_call(w_fourier)
    esum, expand = _const_tables()
    cos_l, sin_l = _rope_tables(S)
    cos_c = jnp.ones((n_ctx, SLOT), F32)
    sin_c = jnp.zeros((n_ctx, SLOT), F32)
    rwt = router_w.T
    rb = router_bias[:, None]

    tm_lat = 512
    tm_ctx = min(256, n_ctx)
    for l in range(depth):
        last = l == depth - 1
        lw = _layer_weights(l, p, mf_all)
        lw["esum"] = esum
        lw["expand"] = expand
        sh1, sc1, g1, sh2, sc2, g2 = _mod6(mod[l], list(range(B)), B)
        csh1, csc1, cg1, csh2, csc2, cg2 = _mod6(mod[l], [B] * B, B)

        q_c, k_c, v_c, wri_c = _premix_call(ctx, csh1, csc1, lw, cos_c, sin_c, tm_ctx)
        q_l, k_l, v_l, wri_l = _premix_call(x, sh1, sc1, lw, cos_l, sin_l, tm_lat)
        k_all = jnp.concatenate([k_c, k_l], axis=1)
        v_all = jnp.concatenate([v_c, v_l], axis=1)
        bounded = _scores_bounded(q_norm[l], k_norm[l])
        o_l = _attention(q_l, k_all, v_all, 512, bounded)
        z_l = _fft_call(wri_l)
        x_mid, f_l, wd_l = _postmix_call(o_l, z_l, x, g1, sh2, sc2, lw, rwt, rb, tm_lat)
        x = _moe_call(f_l, wd_l, x_mid, g2, lw, tm_lat)
        if not last:
            o_c = _attention(q_c, k_c, v_c, tm_ctx, bounded)
            z_c = _dft_small_call(wri_c)
            c_mid, f_c, wd_c = _postmix_call(o_c, z_c, ctx, cg1, csh2, csc2, lw, rwt, rb, tm_ctx)
            ctx = _moe_call(f_c, wd_c, c_mid, cg2, lw, tm_ctx)
    return x
```

```python
import functools
import math

import numpy as np
import jax
import jax.numpy as jnp
from jax import lax
from jax.experimental import pallas as pl
from jax.experimental.pallas import tpu as pltpu

F32 = jnp.float32
BF16 = jnp.bfloat16
HIGHEST = lax.Precision.HIGHEST

D_MODEL = 1024
N_HEADS = 8
QK_NOPE = 64
QK_ROPE = 32
QK_DIM = QK_NOPE + QK_ROPE
V_DIM = 64
Q_LORA = 256
KV_LORA = 128
ROPE_AXIS = QK_ROPE // 2
ROPE_HALF = ROPE_AXIS // 2
ROPE_BASE = 10000.0
GRID_W = 64
F_GROUPS = 4
F_GDIM = 128
F_WIDTH = F_GROUPS * F_GDIM
MLA_WIDTH = N_HEADS * V_DIM
N_EXPERTS = 16
N_EGROUPS = 4
EPG = N_EXPERTS // N_EGROUPS
D_EXPERT = 256
EPS = 1e-6
ROUTE_ROWS = 16

LANES = 128
SLOT = LANES
HSLOTS = N_HEADS * SLOT
VMEM_LIMIT = 56 * 1024 * 1024

C_CQ = 0
C_CKV = C_CQ + Q_LORA
C_KR = C_CKV + KV_LORA
C_KRS = C_KR + SLOT
C_F = C_KRS + SLOT
IN_COLS = C_F + F_WIDTH


def _cparams(sem):
    return pltpu.CompilerParams(dimension_semantics=sem, vmem_limit_bytes=VMEM_LIMIT)


def _full(shape):
    n = len(shape)
    return pl.BlockSpec(shape, lambda *_: (0,) * n)


def _ada_kernel(c_ref, w_ref, b_ref, o_ref):
    c = c_ref[...]
    s = c * jax.nn.sigmoid(c)
    o_ref[0] = jnp.dot(s, w_ref[0], preferred_element_type=F32, precision=HIGHEST) + b_ref[0]


def _ada_call(c8, w_ada, b_ada):
    L, D, W = w_ada.shape
    tn = 1536
    return pl.pallas_call(
        _ada_kernel,
        grid=(L, W // tn),
        in_specs=[
            pl.BlockSpec((8, D), lambda l, j: (0, 0)),
            pl.BlockSpec((1, D, tn), lambda l, j: (l, 0, j)),
            pl.BlockSpec((1, 1, tn), lambda l, j: (l, 0, j)),
        ],
        out_specs=pl.BlockSpec((1, 8, tn), lambda l, j: (l, 0, j)),
        out_shape=jax.ShapeDtypeStruct((L, 8, W), F32),
        compiler_params=_cparams(("parallel", "parallel")),
        name="adaln_mod",
    )(c8, w_ada, b_ada.reshape(L, 1, W))


def _fmat_kernel(cc_ref, sc_ref, wf_ref, o_ref):
    o_ref[...] = jnp.zeros(o_ref.shape, o_ref.dtype)
    for g in range(F_GROUPS):
        wf = wf_ref[0, g]
        mr = jnp.dot(cc_ref[...], wf, preferred_element_type=F32, precision=HIGHEST)
        mi = jnp.dot(sc_ref[...], wf, preferred_element_type=F32, precision=HIGHEST)
        r0 = g * F_GDIM
        o_ref[0, r0:r0 + F_GDIM, r0:r0 + F_GDIM] = mr.astype(o_ref.dtype)
        o_ref[0, r0:r0 + F_GDIM, F_WIDTH + r0:F_WIDTH + r0 + F_GDIM] = mi.astype(o_ref.dtype)


def _fmat_call(w_fourier):
    L = w_fourier.shape[0]
    c = np.arange(F_GDIM)
    ang = 2.0 * np.pi * np.outer(c, c) / F_GDIM
    cc = jnp.asarray((np.cos(ang) / math.sqrt(F_GDIM)).astype(np.float32))
    sc = jnp.asarray((-np.sin(ang) / math.sqrt(F_GDIM)).astype(np.float32))
    return pl.pallas_call(
        _fmat_kernel,
        grid=(L,),
        in_specs=[
            _full((F_GDIM, F_GDIM)),
            _full((F_GDIM, F_GDIM)),
            pl.BlockSpec((1, F_GROUPS, F_GDIM, F_GDIM), lambda l: (l, 0, 0, 0)),
        ],
        out_specs=pl.BlockSpec((1, F_WIDTH, 2 * F_WIDTH), lambda l: (l, 0, 0)),
        out_shape=jax.ShapeDtypeStruct((L, F_WIDTH, 2 * F_WIDTH), BF16),
        compiler_params=_cparams(("parallel",)),
        name="fourier_chan_mats",
    )(cc, sc, w_fourier)


def _rms(x):
    return x * lax.rsqrt(jnp.mean(x * x, axis=-1, keepdims=True) + EPS)


def _tile8(a):
    return jnp.concatenate([a] * N_HEADS, axis=-1)


def _head_sums(x2, e2):
    w = e2.shape[0]
    xb = x2.astype(BF16)
    return jnp.concatenate(
        [jnp.dot(xb[:, c * w:(c + 1) * w], e2, preferred_element_type=F32) for c in range(HSLOTS // w)], axis=-1)


def _premix_kernel(x_ref, sh_ref, sc_ref, gn_ref, win_ref, qa_ref, wq_ref, wqs_ref, kva_ref, wk_ref, wv_ref,
                   e_ref, cos_ref, sin_ref, gq_ref, gqs_ref, gk_ref, gks_ref, mf_ref,
                   q_ref, k_ref, v_ref, wri_ref):
    x = x_ref[0]
    h = _rms(x) * gn_ref[...] * (1.0 + sc_ref[0]) + sh_ref[0]
    proj = jnp.dot(h.astype(BF16), win_ref[...], preferred_element_type=F32)
    cq = proj[:, C_CQ:C_CKV]
    ckv = proj[:, C_CKV:C_KR]
    kr = proj[:, C_KR:C_KRS]
    krs = proj[:, C_KRS:C_F]
    f = proj[:, C_F:IN_COLS]

    cos8 = _tile8(cos_ref[...])
    sin8 = _tile8(sin_ref[...])
    inv_d = 1.0 / QK_DIM

    cqn = (_rms(cq) * qa_ref[...]).astype(BF16)
    qr = jnp.dot(cqn, wq_ref[...], preferred_element_type=F32)
    qs = jnp.dot(cqn, wqs_ref[...], preferred_element_type=F32)
    ssq = _head_sums(qr * qr, e_ref[...])
    rq = lax.rsqrt(ssq * inv_d + EPS)
    q = rq * (qr * (gq_ref[...] * cos8) + qs * (gqs_ref[...] * sin8))
    q_ref[0] = q.astype(q_ref.dtype)

    ckvn = (_rms(ckv) * kva_ref[...]).astype(BF16)
    kraw = jnp.dot(ckvn, wk_ref[...], preferred_element_type=F32) + _tile8(kr)
    ssk = _head_sums(kraw * kraw, e_ref[...])
    rk = lax.rsqrt(ssk * inv_d + EPS)
    k = rk * (kraw * (gk_ref[...] * cos8) + _tile8(krs) * (gks_ref[...] * sin8))
    k_ref[0] = k.astype(k_ref.dtype)
    v_ref[0] = jnp.dot(ckvn, wv_ref[...], preferred_element_type=F32).astype(v_ref.dtype)

    w = jnp.dot(f.astype(BF16), mf_ref[...], preferred_element_type=F32)
    wri_ref[0, 0] = w[:, :F_WIDTH].astype(wri_ref.dtype)
    wri_ref[0, 1] = w[:, F_WIDTH:].astype(wri_ref.dtype)


def _premix_call(x, sh, sc, lw, cos_t, sin_t, tm):
    B, N, D = x.shape
    row = lambda b, i: (b, i, 0)
    vec = lambda b, i: (b, 0, 0)
    in_specs = [
        pl.BlockSpec((1, tm, D), row),
        pl.BlockSpec((1, 1, D), vec),
        pl.BlockSpec((1, 1, D), vec),
        _full((1, D)),
        _full((D, IN_COLS)),
        _full((1, Q_LORA)),
        _full((Q_LORA, HSLOTS)),
        _full((Q_LORA, HSLOTS)),
        _full((1, KV_LORA)),
        _full((KV_LORA, HSLOTS)),
        _full((KV_LORA, MLA_WIDTH)),
        _full((2 * SLOT, 2 * SLOT)),
        pl.BlockSpec((tm, SLOT), lambda b, i: (i, 0)),
        pl.BlockSpec((tm, SLOT), lambda b, i: (i, 0)),
        _full((1, HSLOTS)),
        _full((1, HSLOTS)),
        _full((1, HSLOTS)),
        _full((1, HSLOTS)),
        _full((F_WIDTH, 2 * F_WIDTH)),
    ]
    out_specs = [
        pl.BlockSpec((1, tm, HSLOTS), row),
        pl.BlockSpec((1, tm, HSLOTS), row),
        pl.BlockSpec((1, tm, MLA_WIDTH), row),
        pl.BlockSpec((1, 2, tm, F_WIDTH), lambda b, i: (b, 0, i, 0)),
    ]
    out_shape = [
        jax.ShapeDtypeStruct((B, N, HSLOTS), BF16),
        jax.ShapeDtypeStruct((B, N, HSLOTS), BF16),
        jax.ShapeDtypeStruct((B, N, MLA_WIDTH), BF16),
        jax.ShapeDtypeStruct((B, 2, N, F_WIDTH), BF16),
    ]
    return pl.pallas_call(
        _premix_kernel,
        grid=(B, N // tm),
        in_specs=in_specs,
        out_specs=out_specs,
        out_shape=out_shape,
        compiler_params=_cparams(("parallel", "parallel")),
        name="premix_proj",
    )(x, sh, sc, lw["gmix"], lw["win"], lw["qa"], lw["wq"], lw["wqs"], lw["kva"], lw["wk"], lw["wv"],
      lw["esum"], cos_t, sin_t, lw["gq"], lw["gqs"], lw["gk"], lw["gks"], lw["mf"])


def _attn_kernel(q_ref, k_ref, v_ref, ot_ref, vt_scr, qt_scr, *, n_kt, bounded, lookahead):
    tq = q_ref.shape[1]
    tk = ATTN_TK

    @pl.when(pl.program_id(2) == 0)
    def _():
        odd = (pl.program_id(1) % 2) == 1
        ones = jnp.ones((VT_ROWS - V_DIM, tk), BF16)
        for t in range(n_kt):
            vt_pair = v_ref[0, t * tk:(t + 1) * tk, :].T
            vt_scr[t, :V_DIM, :] = jnp.where(odd, vt_pair[V_DIM:], vt_pair[:V_DIM])
            vt_scr[t, V_DIM:, :] = ones

    qt_scr[...] = q_ref[0].T
    qt = qt_scr[...]

    def scores(t):
        off = t * tk if isinstance(t, int) else pl.multiple_of(t * tk, tk)
        return jnp.dot(k_ref[0, pl.ds(off, tk), :], qt, preferred_element_type=F32)

    def pv(t, p):
        return jnp.dot(vt_scr[t], p.astype(BF16), preferred_element_type=F32)

    if bounded:
        acc = jnp.zeros((VT_ROWS, tq), F32)
        ahead = [scores(t) for t in range(min(lookahead, n_kt))]
        for t in range(n_kt):
            if t + lookahead < n_kt:
                ahead.append(scores(t + lookahead))
            acc = acc + pv(t, jnp.exp2(ahead.pop(0)))
    else:
        unroll = _attn_unroll(n_kt)

        def body(g, carry):
            m, acc, s = carry
            for u in range(unroll):
                t = g * unroll + u
                s_next = scores(jnp.minimum(t + 1, n_kt - 1))
                m_new = jnp.maximum(m, jnp.max(s, axis=0, keepdims=True))
                acc = jnp.exp2(m - m_new) * acc + pv(t, jnp.exp2(s - m_new))
                m, s = m_new, s_next
            return m, acc, s

        init = (jnp.full((1, tq), -1e30, F32), jnp.zeros((VT_ROWS, tq), F32), scores(0))
        _, acc, _ = lax.fori_loop(0, n_kt // unroll, body, init)
    ot_ref[0] = (acc[:V_DIM] / acc[V_DIM:V_DIM + 1]).astype(ot_ref.dtype)


ATTN_TK = 256
VT_ROWS = V_DIM + 16
ATTN_LAYER_CFG = ((512, 2), (512, 3), (1024, 2), (512, 1))
ATTN_SCORE_BOUND = 60.0


def _attn_unroll(n_kt):
    for u in (13, 9, 8, 5, 4, 3, 2):
        if n_kt % u == 0:
            return u
    return 1


def _attn_call(q, k, v, tq, bounded, lookahead):
    B, Nq, _ = q.shape
    Nk = k.shape[1]
    n_kt = Nk // ATTN_TK
    kern = functools.partial(_attn_kernel, n_kt=n_kt, bounded=bounded, lookahead=lookahead)
    return pl.pallas_call(
        kern,
        grid=(B, N_HEADS, Nq // tq),
        in_specs=[
            pl.BlockSpec((1, tq, SLOT), lambda b, h, i: (b, i, h)),
            pl.BlockSpec((1, Nk, SLOT), lambda b, h, i: (b, 0, h)),
            pl.BlockSpec((1, Nk, 2 * V_DIM), lambda b, h, i: (b, 0, h // 2)),
        ],
        out_specs=pl.BlockSpec((1, V_DIM, tq), lambda b, h, i: (b, h, i)),
        out_shape=jax.ShapeDtypeStruct((B, MLA_WIDTH, Nq), BF16),
        scratch_shapes=[pltpu.VMEM((n_kt, VT_ROWS, ATTN_TK), BF16), pltpu.VMEM((SLOT, tq), BF16)],
        compiler_params=_cparams(("arbitrary", "arbitrary", "arbitrary")),
        name="mla_attention_bounded" if bounded else "mla_attention_online",
    )(q, k, v)


def _attention(q, k, v, tq, scores_bounded, lookahead=2):
    return lax.cond(scores_bounded,
                    lambda a, b, c: _attn_call(a, b, c, tq, True, lookahead),
                    lambda a, b, c: _attn_call(a, b, c, tq, False, 1),
                    q, k, v)


def _scores_bounded(q_gain, k_gain):
    bound = math.sqrt(QK_DIM) * math.log2(math.e) * jnp.max(jnp.abs(q_gain)) * jnp.max(jnp.abs(k_gain))
    return bound <= ATTN_SCORE_BOUND


def _fft1_kernel(f1_ref, tr_ref, ti_ref, w_ref, y_ref, *, n1, n2_per):
    y = jnp.dot(f1_ref[...], w_ref[0], preferred_element_type=F32)
    yr = y[:n1]
    yi = y[n1:]
    for t in range(n2_per):
        tr = tr_ref[:, t * LANES:(t + 1) * LANES]
        ti = ti_ref[:, t * LANES:(t + 1) * LANES]
        tr4 = jnp.concatenate([tr] * F_GROUPS, axis=-1)
        ti4 = jnp.concatenate([ti] * F_GROUPS, axis=-1)
        a = yr[:, t * F_WIDTH:(t + 1) * F_WIDTH]
        b = yi[:, t * F_WIDTH:(t + 1) * F_WIDTH]
        y_ref[0, :n1, t * F_WIDTH:(t + 1) * F_WIDTH] = (a * tr4 - b * ti4).astype(y_ref.dtype)
        y_ref[0, n1:, t * F_WIDTH:(t + 1) * F_WIDTH] = (a * ti4 + b * tr4).astype(y_ref.dtype)


def _fft2_kernel(c2_ref, s2_ref, y_ref, o_ref, *, kc):
    for i in range(kc):
        re = jnp.dot(c2_ref[...], y_ref[0, 0, i], preferred_element_type=F32)
        im = jnp.dot(s2_ref[...], y_ref[0, 1, i], preferred_element_type=F32)
        o_ref[0, :, i * F_WIDTH:(i + 1) * F_WIDTH] = (re + im).astype(o_ref.dtype)


def _fft_consts(n):
    n2 = LANES
    n1 = n // n2
    a1 = 2.0 * np.pi * np.outer(np.arange(n1), np.arange(n1)) / n1
    c1, s1 = np.cos(a1), np.sin(a1)
    f1 = np.block([[c1, s1], [-s1, c1]]).astype(np.float32)
    at = 2.0 * np.pi * np.outer(np.arange(n1), np.arange(n2)) / n
    tr = np.repeat(np.cos(at), LANES, axis=1).astype(np.float32)
    ti = np.repeat(-np.sin(at), LANES, axis=1).astype(np.float32)
    a2 = 2.0 * np.pi * np.outer(np.arange(n2), np.arange(n2)) / n2
    scale = 1.0 / math.sqrt(n)
    c2 = (np.cos(a2) * scale).astype(np.float32)
    s2 = (np.sin(a2) * scale).astype(np.float32)
    return n1, n2, f1, tr, ti, c2, s2


def _fft_call(wri):
    B, _, N, _ = wri.shape
    n1, n2, f1, tr, ti, c2, s2 = _fft_consts(N)
    cols = n2 * F_WIDTH
    n2_per = 4
    tc = n2_per * F_WIDTH
    y = pl.pallas_call(
        functools.partial(_fft1_kernel, n1=n1, n2_per=n2_per),
        grid=(B, cols // tc),
        in_specs=[
            _full((2 * n1, 2 * n1)),
            pl.BlockSpec((n1, n2_per * LANES), lambda b, j: (0, j)),
            pl.BlockSpec((n1, n2_per * LANES), lambda b, j: (0, j)),
            pl.BlockSpec((1, 2 * n1, tc), lambda b, j: (b, 0, j)),
        ],
        out_specs=pl.BlockSpec((1, 2 * n1, tc), lambda b, j: (b, 0, j)),
        out_shape=jax.ShapeDtypeStruct((B, 2 * n1, cols), BF16),
        compiler_params=_cparams(("parallel", "parallel")),
        name="fft_stage1",
    )(jnp.asarray(f1, BF16), jnp.asarray(tr), jnp.asarray(ti), wri.reshape(B, 2 * n1, cols))
    kc = min(8, n1)
    out = pl.pallas_call(
        functools.partial(_fft2_kernel, kc=kc),
        grid=(B, n1 // kc),
        in_specs=[
            _full((n2, n2)),
            _full((n2, n2)),
            pl.BlockSpec((1, 2, kc, n2, F_WIDTH), lambda b, j: (b, 0, j, 0, 0)),
        ],
        out_specs=pl.BlockSpec((1, n2, kc * F_WIDTH), lambda b, j: (b, 0, j)),
        out_shape=jax.ShapeDtypeStruct((B, n2, n1 * F_WIDTH), BF16),
        compiler_params=_cparams(("parallel", "parallel")),
        name="fft_stage2",
    )(jnp.asarray(c2, BF16), jnp.asarray(s2, BF16), y.reshape(B, 2, n1, n2, F_WIDTH))
    return out.reshape(B, N, F_WIDTH)


def _dft_small_kernel(c_ref, s_ref, w_ref, o_ref):
    re = jnp.dot(c_ref[...], w_ref[0, 0], preferred_element_type=F32)
    im = jnp.dot(s_ref[...], w_ref[0, 1], preferred_element_type=F32)
    o_ref[0] = (re + im).astype(o_ref.dtype)


def _dft_small_call(wri):
    B, _, N, _ = wri.shape
    a = 2.0 * np.pi * np.outer(np.arange(N), np.arange(N)) / N
    c = (np.cos(a) / math.sqrt(N)).astype(np.float32)
    s = (np.sin(a) / math.sqrt(N)).astype(np.float32)
    return pl.pallas_call(
        _dft_small_kernel,
        grid=(B,),
        in_specs=[_full((N, N)), _full((N, N)), pl.BlockSpec((1, 2, N, F_WIDTH), lambda b: (b, 0, 0, 0))],
        out_specs=pl.BlockSpec((1, N, F_WIDTH), lambda b: (b, 0, 0)),
        out_shape=jax.ShapeDtypeStruct((B, N, F_WIDTH), BF16),
        compiler_params=_cparams(("parallel",)),
        name="dft_small",
    )(jnp.asarray(c, BF16), jnp.asarray(s, BF16), wri)


def _router_rows(logits, bias):
    t = logits.shape[1]
    s = jax.nn.sigmoid(logits)
    b = s + bias
    srow = [s[e:e + 1, :] for e in range(N_EXPERTS)]
    brow = [b[e:e + 1, :] for e in range(N_EXPERTS)]
    gscore = []
    for g in range(N_EGROUPS):
        b0, b1, b2, b3 = brow[EPG * g:EPG * g + EPG]
        m1, n1 = jnp.maximum(b0, b1), jnp.minimum(b0, b1)
        m2, n2 = jnp.maximum(b2, b3), jnp.minimum(b2, b3)
        top1 = jnp.maximum(m1, m2)
        top2 = jnp.maximum(jnp.minimum(m1, m2), jnp.maximum(n1, n2))
        gscore.append(top1 + top2)
    best = gscore[0]
    gi = jnp.zeros((1, t), jnp.int32)
    for g in range(1, N_EGROUPS):
        take = gscore[g] > best
        gi = jnp.where(take, g, gi)
        best = jnp.where(take, gscore[g], best)
    ib, isc = [], []
    for i in range(EPG):
        vb, vs = brow[i], srow[i]
        for g in range(1, N_EGROUPS):
            sel = gi == g
            vb = jnp.where(sel, brow[EPG * g + i], vb)
            vs = jnp.where(sel, srow[EPG * g + i], vs)
        ib.append(vb)
        isc.append(vs)
    w = []
    for i in range(EPG):
        rank = jnp.zeros((1, t), jnp.int32)
        for j in range(EPG):
            if j == i:
                continue
            ahead = (ib[j] > ib[i]) | ((ib[j] == ib[i]) & (j < i))
            rank = rank + ahead.astype(jnp.int32)
        w.append(jnp.where(rank < 2, isc[i], 0.0))
    tot = (w[0] + w[1]) + (w[2] + w[3])
    inv = 1.0 / tot
    ridx = lax.broadcasted_iota(jnp.int32, (ROUTE_ROWS, t), 0)
    out = jnp.where(ridx == 2 * EPG, gi.astype(F32), 0.0)
    for i in range(EPG):
        out = jnp.where((ridx == i) | (ridx == EPG + i), w[i] * inv, out)
    return out


def _postmix_kernel(ot_ref, z_ref, x_ref, g1_ref, sh_ref, sc_ref, gn_ref, wo_ref, rw_ref, rb_ref,
                    xo_ref, f_ref, wd_ref):
    mix = lax.dot_general(ot_ref[0], wo_ref[:MLA_WIDTH, :], (((0,), (0,)), ((), ())), preferred_element_type=F32)
    mix = mix + jnp.dot(z_ref[0], wo_ref[MLA_WIDTH:, :], preferred_element_type=F32)
    xn = x_ref[0] + g1_ref[0] * mix
    xo_ref[0] = xn
    f = _rms(xn) * gn_ref[...] * (1.0 + sc_ref[0]) + sh_ref[0]
    f_ref[0] = f.astype(f_ref.dtype)
    logits = lax.dot_general(rw_ref[...], f, (((1,), (1,)), ((), ())), preferred_element_type=F32,
                             precision=HIGHEST)
    wd_ref[0] = _router_rows(logits, rb_ref[...])


def _postmix_call(o, z, x, g1, sh, sc, lw, rwt, rb, tm):
    B, N, D = x.shape
    row = lambda b, i: (b, i, 0)
    vec = lambda b, i: (b, 0, 0)
    return pl.pallas_call(
        _postmix_kernel,
        grid=(B, N // tm),
        in_specs=[
            pl.BlockSpec((1, MLA_WIDTH, tm), lambda b, i: (b, 0, i)),
            pl.BlockSpec((1, tm, F_WIDTH), row),
            pl.BlockSpec((1, tm, D), row),
            pl.BlockSpec((1, 1, D), vec),
            pl.BlockSpec((1, 1, D), vec),
            pl.BlockSpec((1, 1, D), vec),
            _full((1, D)),
            _full((D, D)),
            _full((N_EXPERTS, D)),
            _full((N_EXPERTS, 1)),
        ],
        out_specs=[
            pl.BlockSpec((1, tm, D), row),
            pl.BlockSpec((1, tm, D), row),
            pl.BlockSpec((1, ROUTE_ROWS, tm), lambda b, i: (b, 0, i)),
        ],
        out_shape=[
            jax.ShapeDtypeStruct((B, N, D), F32),
            jax.ShapeDtypeStruct((B, N, D), BF16),
            jax.ShapeDtypeStruct((B, ROUTE_ROWS, N), F32),
        ],
        compiler_params=_cparams(("parallel", "parallel")),
        name="postmix_router",
    )(o, z, x, g1, sh, sc, lw["gffn"], lw["wout"], rwt, rb)


def _expert_mlp(xb, wexp, g, wgate_ref, wup_ref, wdown_ref):
    hg = jnp.dot(xb, wgate_ref[g], preferred_element_type=F32)
    hu = jnp.dot(xb, wup_ref[g], preferred_element_type=F32)
    h = (hg * jax.nn.sigmoid(hg)) * hu * wexp
    return jnp.dot(h.astype(BF16), wdown_ref[g], preferred_element_type=F32)


def _moe_kernel(nch_ref, f_ref, rt_ref, x_ref, g2_ref, tri_ref, ex_ref, wgate_ref, wup_ref, wdown_ref, xo_ref, y_scr):
    tm = f_ref.shape[0]
    tile = pl.program_id(0)
    f = f_ref[...]
    rt = rt_ref[0]
    gid = rt[2 * EPG:2 * EPG + 1, :]
    r16 = lax.broadcasted_iota(jnp.int32, (ROUTE_ROWS, tm), 0)
    hi = rt.astype(BF16).astype(F32)
    whl = jnp.where(r16 < EPG, hi, jnp.where(r16 < 2 * EPG, rt - hi, 0.0)).astype(BF16)
    onehot = (r16.astype(F32) == gid).astype(BF16)
    before = jnp.dot(onehot, tri_ref[...], preferred_element_type=F32)
    rank = jnp.sum(onehot.astype(F32) * before, axis=0, keepdims=True)
    slot0 = lax.broadcasted_iota(jnp.int32, (MOE_CHUNK, tm), 0)
    y_scr[...] = jnp.zeros(y_scr.shape, y_scr.dtype)
    for g in range(N_EGROUPS):
        rank_g = jnp.where(gid == g, rank, -1.0)

        def chunk(c, carry, g=g, rank_g=rank_g):
            sel = ((slot0 + c * MOE_CHUNK).astype(F32) == rank_g).astype(BF16)
            xg = jnp.dot(sel, f, preferred_element_type=F32).astype(BF16)
            wr = lax.dot_general(sel, whl, (((1,), (1,)), ((), ())), preferred_element_type=F32).astype(BF16)
            wexp = jnp.dot(wr, ex_ref[...], preferred_element_type=F32)
            yg = _expert_mlp(xg, wexp, g, wgate_ref, wup_ref, wdown_ref)
            y_scr[...] += lax.dot_general(sel, yg.astype(BF16), (((0,), (0,)), ((), ())),
                                          preferred_element_type=F32)
            return carry

        lax.fori_loop(0, nch_ref[tile * N_EGROUPS + g], chunk, 0)
    xo_ref[...] = x_ref[...] + g2_ref[0] * y_scr[...]


MOE_CHUNK = 128


def _moe_call(f, rt, x, g2, lw, tm):
    B, N, D = x.shape
    GW = EPG * D_EXPERT
    tpb = N // tm
    n_tiles = B * tpb
    gid = rt[:, 2 * EPG, :].reshape(n_tiles, tm)
    counts = jnp.stack([jnp.sum(gid == g, axis=1) for g in range(N_EGROUPS)], axis=1)
    nch = ((counts + MOE_CHUNK - 1) // MOE_CHUNK).astype(jnp.int32).reshape(n_tiles * N_EGROUPS)
    tri = jnp.asarray(np.triu(np.ones((tm, tm), np.float32), 1), BF16)
    resident = dict(pipeline_mode=pl.Buffered(1))
    grid_spec = pltpu.PrefetchScalarGridSpec(
        num_scalar_prefetch=1,
        grid=(n_tiles,),
        in_specs=[
            pl.BlockSpec((tm, D), lambda i, o: (i, 0)),
            pl.BlockSpec((1, ROUTE_ROWS, tm), lambda i, o: (i // tpb, 0, i % tpb)),
            pl.BlockSpec((tm, D), lambda i, o: (i, 0)),
            pl.BlockSpec((1, 1, D), lambda i, o: (i // tpb, 0, 0)),
            pl.BlockSpec((tm, tm), lambda i, o: (0, 0), **resident),
            pl.BlockSpec((ROUTE_ROWS, GW), lambda i, o: (0, 0), **resident),
            pl.BlockSpec((N_EGROUPS, D, GW), lambda i, o: (0, 0, 0), **resident),
            pl.BlockSpec((N_EGROUPS, D, GW), lambda i, o: (0, 0, 0), **resident),
            pl.BlockSpec((N_EGROUPS, GW, D), lambda i, o: (0, 0, 0), **resident),
        ],
        out_specs=pl.BlockSpec((tm, D), lambda i, o: (i, 0)),
        scratch_shapes=[pltpu.VMEM((tm, D), F32)],
    )
    out = pl.pallas_call(
        _moe_kernel,
        grid_spec=grid_spec,
        out_shape=jax.ShapeDtypeStruct((B * N, D), F32),
        compiler_params=_cparams(("arbitrary",)),
        name="moe_experts",
    )(nch, f.reshape(B * N, D), rt, x.reshape(B * N, D), g2, tri, lw["expand"], lw["wgate"], lw["wup"], lw["wdown"])
    return out.reshape(B, N, D)


def _rope_partner():
    p = np.arange(SLOT)
    for j in range(ROPE_HALF):
        for base in (QK_NOPE, QK_NOPE + ROPE_AXIS):
            p[base + j] = base + ROPE_HALF + j
            p[base + ROPE_HALF + j] = base + j
    return p


def _rope_tables(n):
    rows = n // GRID_W
    row = jnp.repeat(jnp.arange(rows, dtype=F32), GRID_W)
    col = jnp.tile(jnp.arange(GRID_W, dtype=F32), rows)
    freqs = ROPE_BASE ** (-jnp.arange(ROPE_HALF, dtype=F32) / ROPE_HALF)
    ar, ac = row[:, None] * freqs, col[:, None] * freqs
    one = jnp.ones((n, QK_NOPE), F32)
    zero = jnp.zeros((n, QK_NOPE), F32)
    pad1 = jnp.ones((n, SLOT - QK_DIM), F32)
    pad0 = jnp.zeros((n, SLOT - QK_DIM), F32)
    cos_t = jnp.concatenate([one, jnp.cos(ar), jnp.cos(ar), jnp.cos(ac), jnp.cos(ac), pad1], axis=1)
    sin_t = jnp.concatenate([zero, -jnp.sin(ar), jnp.sin(ar), -jnp.sin(ac), jnp.sin(ac), pad0], axis=1)
    return cos_t, sin_t


def _slot_cols(w, width):
    kdim = w.shape[0]
    w = w.reshape(kdim, N_HEADS, width)
    return jnp.pad(w, ((0, 0), (0, 0), (0, SLOT - width))).reshape(kdim, HSLOTS)


def _layer_weights(l, p, mf_all):
    partner = _rope_partner()
    w_in = p["w_in"][l]
    kr_cols = w_in[:, Q_LORA + KV_LORA:Q_LORA + KV_LORA + QK_ROPE]
    kr_slot = jnp.pad(kr_cols, ((0, 0), (QK_NOPE, SLOT - QK_DIM)))
    kr_swap = jnp.where((np.arange(SLOT) >= QK_NOPE) & (np.arange(SLOT) < QK_DIM), kr_slot[:, partner], 0.0)
    win = jnp.concatenate([w_in[:, :Q_LORA + KV_LORA], kr_slot, kr_swap, w_in[:, Q_LORA + KV_LORA + QK_ROPE:]], axis=1)

    rope_lane = (np.arange(SLOT) >= QK_NOPE) & (np.arange(SLOT) < QK_DIM)
    wq = _slot_cols(p["w_q_b"][l], QK_DIM)
    wq3 = wq.reshape(Q_LORA, N_HEADS, SLOT)
    wqs = jnp.where(rope_lane, wq3[:, :, partner], 0.0).reshape(Q_LORA, HSLOTS)

    wkv = p["w_kv_b"][l].reshape(KV_LORA, N_HEADS, QK_NOPE + V_DIM)
    wk = jnp.pad(wkv[:, :, :QK_NOPE], ((0, 0), (0, 0), (0, SLOT - QK_NOPE))).reshape(KV_LORA, HSLOTS)
    wv = wkv[:, :, QK_NOPE:].reshape(KV_LORA, MLA_WIDTH)

    def gains(g, scale):
        gp = jnp.pad(g, (0, SLOT - QK_DIM))
        gs = jnp.where(rope_lane, gp[partner], 0.0)
        return (jnp.tile(gp, N_HEADS) * scale)[None, :], (jnp.tile(gs, N_HEADS) * scale)[None, :]

    gq, gqs = gains(p["q_norm"][l], QK_DIM ** -0.5 * math.log2(math.e))
    gk, gks = gains(p["k_norm"][l], 1.0)

    def group_cols(w):
        return w.reshape(N_EGROUPS, EPG, D_MODEL, D_EXPERT).transpose(0, 2, 1, 3).reshape(
            N_EGROUPS, D_MODEL, EPG * D_EXPERT).astype(BF16)

    return {
        "gmix": p["norm_mix"][l][None, :],
        "gffn": p["norm_ffn"][l][None, :],
        "win": win.astype(BF16),
        "qa": p["q_a_norm"][l][None, :],
        "wq": wq.astype(BF16),
        "wqs": wqs.astype(BF16),
        "kva": p["kv_a_norm"][l][None, :],
        "wk": wk.astype(BF16),
        "wv": wv.astype(BF16),
        "gq": gq, "gqs": gqs, "gk": gk, "gks": gks,
        "mf": mf_all[l],
        "wout": p["w_out"][l].astype(BF16),
        "wgate": group_cols(p["w_gate"][l]),
        "wup": group_cols(p["w_up"][l]),
        "wdown": p["w_down"][l].reshape(N_EGROUPS, EPG * D_EXPERT, D_MODEL).astype(BF16),
    }


def _const_tables():
    lane_head = np.arange(2 * SLOT) // SLOT
    esum = (lane_head[:, None] == lane_head[None, :]).astype(np.float32)
    ex = np.zeros((ROUTE_ROWS, EPG * D_EXPERT), np.float32)
    for i in range(EPG):
        ex[i, i * D_EXPERT:(i + 1) * D_EXPERT] = 1.0
        ex[EPG + i, i * D_EXPERT:(i + 1) * D_EXPERT] = 1.0
    return jnp.asarray(esum, BF16), jnp.asarray(ex, BF16)


def _mod6(mod_l, rows, B):
    m = mod_l[jnp.asarray(rows)]
    return [m[:, i * D_MODEL:(i + 1) * D_MODEL][:, None, :] for i in range(6)]


def kernel(x, c, ctx, c_ctx, w_ada, b_ada, norm_mix, norm_ffn, w_in, q_a_norm, w_q_b, kv_a_norm, w_kv_b, q_norm,
           k_norm, w_fourier, w_out, router_w, router_bias, w_gate, w_up, w_down):
    B, S, D = x.shape
    n_ctx = ctx.shape[1]
    depth = w_ada.shape[0]
    p = dict(w_in=w_in, q_a_norm=q_a_norm, w_q_b=w_q_b, kv_a_norm=kv_a_norm, w_kv_b=w_kv_b, q_norm=q_norm,
             k_norm=k_norm, norm_mix=norm_mix, norm_ffn=norm_ffn, w_out=w_out, w_gate=w_gate, w_up=w_up,
             w_down=w_down)

    c8 = jnp.concatenate([c, c_ctx[None, :], jnp.zeros((8 - B - 1, D), F32)], axis=0)
    mod = _ada_call(c8, w_ada, b_ada)
    mf_all = _fmat_call(w_fourier)
    esum, expand = _const_tables()
    cos_l, sin_l = _rope_tables(S)
    cos_c = jnp.ones((n_ctx, SLOT), F32)
    sin_c = jnp.zeros((n_ctx, SLOT), F32)
    rwt = router_w.T
    rb = router_bias[:, None]

    tm_lat = 512
    tm_ctx = min(256, n_ctx)
    for l in range(depth):
        last = l == depth - 1
        lw = _layer_weights(l, p, mf_all)
        lw["esum"] = esum
        lw["expand"] = expand
        sh1, sc1, g1, sh2, sc2, g2 = _mod6(mod[l], list(range(B)), B)
        csh1, csc1, cg1, csh2, csc2, cg2 = _mod6(mod[l], [B] * B, B)

        q_c, k_c, v_c, wri_c = _premix_call(ctx, csh1, csc1, lw, cos_c, sin_c, tm_ctx)
        q_l, k_l, v_l, wri_l = _premix_call(x, sh1, sc1, lw, cos_l, sin_l, tm_lat)
        k_all = jnp.concatenate([k_c, k_l], axis=1)
        v_all = jnp.concatenate([v_c, v_l], axis=1)
        bounded = _scores_bounded(q_norm[l], k_norm[l])
        tq_l, la_l = ATTN_LAYER_CFG[l % len(ATTN_LAYER_CFG)]
        o_l = _attention(q_l, k_all, v_all, tq_l, bounded, la_l)
        z_l = _fft_call(wri_l)
        x_mid, f_l, wd_l = _postmix_call(o_l, z_l, x, g1, sh2, sc2, lw, rwt, rb, tm_lat)
        x = _moe_call(f_l, wd_l, x_mid, g2, lw, tm_lat)
        if not last:
            o_c = _attention(q_c, k_c, v_c, tm_ctx, bounded)
            z_c = _dft_small_call(wri_c)
            c_mid, f_c, wd_c = _postmix_call(o_c, z_c, ctx, cg1, csh2, csc2, lw, rwt, rb, tm_ctx)
            ctx = _moe_call(f_c, wd_c, c_mid, cg2, lw, tm_ctx)
    return x
```

```python
import functools
import math

import numpy as np
import jax
import jax.numpy as jnp
from jax import lax
from jax.experimental import pallas as pl
from jax.experimental.pallas import tpu as pltpu

F32 = jnp.float32
BF16 = jnp.bfloat16
HIGHEST = lax.Precision.HIGHEST

D_MODEL = 1024
N_HEADS = 8
QK_NOPE = 64
QK_ROPE = 32
QK_DIM = QK_NOPE + QK_ROPE
V_DIM = 64
Q_LORA = 256
KV_LORA = 128
ROPE_AXIS = QK_ROPE // 2
ROPE_HALF = ROPE_AXIS // 2
ROPE_BASE = 10000.0
GRID_W = 64
F_GROUPS = 4
F_GDIM = 128
F_WIDTH = F_GROUPS * F_GDIM
MLA_WIDTH = N_HEADS * V_DIM
N_EXPERTS = 16
N_EGROUPS = 4
EPG = N_EXPERTS // N_EGROUPS
D_EXPERT = 256
EPS = 1e-6
ROUTE_ROWS = 16

LANES = 128
SLOT = LANES
HSLOTS = N_HEADS * SLOT
VMEM_LIMIT = 56 * 1024 * 1024

C_CQ = 0
C_CKV = C_CQ + Q_LORA
C_KR = C_CKV + KV_LORA
C_KRS = C_KR + SLOT
C_F = C_KRS + SLOT
IN_COLS = C_F + F_WIDTH


def _cparams(sem):
    return pltpu.CompilerParams(dimension_semantics=sem, vmem_limit_bytes=VMEM_LIMIT)


def _full(shape):
    n = len(shape)
    return pl.BlockSpec(shape, lambda *_: (0,) * n)


def _ada_kernel(c_ref, w_ref, b_ref, o_ref):
    c = c_ref[...]
    s = c * jax.nn.sigmoid(c)
    o_ref[0] = jnp.dot(s, w_ref[0], preferred_element_type=F32, precision=HIGHEST) + b_ref[0]


def _ada_call(c8, w_ada, b_ada):
    L, D, W = w_ada.shape
    tn = 1536
    return pl.pallas_call(
        _ada_kernel,
        grid=(L, W // tn),
        in_specs=[
            pl.BlockSpec((8, D), lambda l, j: (0, 0)),
            pl.BlockSpec((1, D, tn), lambda l, j: (l, 0, j)),
            pl.BlockSpec((1, 1, tn), lambda l, j: (l, 0, j)),
        ],
        out_specs=pl.BlockSpec((1, 8, tn), lambda l, j: (l, 0, j)),
        out_shape=jax.ShapeDtypeStruct((L, 8, W), F32),
        compiler_params=_cparams(("parallel", "parallel")),
        name="adaln_mod",
    )(c8, w_ada, b_ada.reshape(L, 1, W))


def _fmat_kernel(cc_ref, sc_ref, wf_ref, o_ref):
    o_ref[...] = jnp.zeros(o_ref.shape, o_ref.dtype)
    for g in range(F_GROUPS):
        wf = wf_ref[0, g]
        mr = jnp.dot(cc_ref[...], wf, preferred_element_type=F32, precision=HIGHEST)
        mi = jnp.dot(sc_ref[...], wf, preferred_element_type=F32, precision=HIGHEST)
        r0 = g * F_GDIM
        o_ref[0, r0:r0 + F_GDIM, r0:r0 + F_GDIM] = mr.astype(o_ref.dtype)
        o_ref[0, r0:r0 + F_GDIM, F_WIDTH + r0:F_WIDTH + r0 + F_GDIM] = mi.astype(o_ref.dtype)


def _fmat_call(w_fourier):
    L = w_fourier.shape[0]
    c = np.arange(F_GDIM)
    ang = 2.0 * np.pi * np.outer(c, c) / F_GDIM
    cc = jnp.asarray((np.cos(ang) / math.sqrt(F_GDIM)).astype(np.float32))
    sc = jnp.asarray((-np.sin(ang) / math.sqrt(F_GDIM)).astype(np.float32))
    return pl.pallas_call(
        _fmat_kernel,
        grid=(L,),
        in_specs=[
            _full((F_GDIM, F_GDIM)),
            _full((F_GDIM, F_GDIM)),
            pl.BlockSpec((1, F_GROUPS, F_GDIM, F_GDIM), lambda l: (l, 0, 0, 0)),
        ],
        out_specs=pl.BlockSpec((1, F_WIDTH, 2 * F_WIDTH), lambda l: (l, 0, 0)),
        out_shape=jax.ShapeDtypeStruct((L, F_WIDTH, 2 * F_WIDTH), BF16),
        compiler_params=_cparams(("parallel",)),
        name="fourier_chan_mats",
    )(cc, sc, w_fourier)


def _rms(x):
    return x * lax.rsqrt(jnp.mean(x * x, axis=-1, keepdims=True) + EPS)


def _tile8(a):
    return jnp.concatenate([a] * N_HEADS, axis=-1)


def _head_sums(x2, e2):
    w = e2.shape[0]
    xb = x2.astype(BF16)
    return jnp.concatenate(
        [jnp.dot(xb[:, c * w:(c + 1) * w], e2, preferred_element_type=F32) for c in range(HSLOTS // w)], axis=-1)


def _premix_kernel(x_ref, sh_ref, sc_ref, gn_ref, win_ref, qa_ref, wq_ref, wqs_ref, kva_ref, wk_ref, wv_ref,
                   e_ref, cos_ref, sin_ref, gq_ref, gqs_ref, gk_ref, gks_ref, mf_ref,
                   q_ref, k_ref, v_ref, wri_ref):
    x = x_ref[0]
    h = _rms(x) * gn_ref[...] * (1.0 + sc_ref[0]) + sh_ref[0]
    proj = jnp.dot(h.astype(BF16), win_ref[...], preferred_element_type=F32)
    cq = proj[:, C_CQ:C_CKV]
    ckv = proj[:, C_CKV:C_KR]
    kr = proj[:, C_KR:C_KRS]
    krs = proj[:, C_KRS:C_F]
    f = proj[:, C_F:IN_COLS]

    cos8 = _tile8(cos_ref[...])
    sin8 = _tile8(sin_ref[...])
    inv_d = 1.0 / QK_DIM

    cqn = (_rms(cq) * qa_ref[...]).astype(BF16)
    qr = jnp.dot(cqn, wq_ref[...], preferred_element_type=F32)
    qs = jnp.dot(cqn, wqs_ref[...], preferred_element_type=F32)
    ssq = _head_sums(qr * qr, e_ref[...])
    rq = lax.rsqrt(ssq * inv_d + EPS)
    q = rq * (qr * (gq_ref[...] * cos8) + qs * (gqs_ref[...] * sin8))
    q_ref[0] = q.astype(q_ref.dtype)

    ckvn = (_rms(ckv) * kva_ref[...]).astype(BF16)
    kraw = jnp.dot(ckvn, wk_ref[...], preferred_element_type=F32) + _tile8(kr)
    ssk = _head_sums(kraw * kraw, e_ref[...])
    rk = lax.rsqrt(ssk * inv_d + EPS)
    k = rk * (kraw * (gk_ref[...] * cos8) + _tile8(krs) * (gks_ref[...] * sin8))
    k_ref[0] = k.astype(k_ref.dtype)
    v_ref[0] = jnp.dot(ckvn, wv_ref[...], preferred_element_type=F32).astype(v_ref.dtype)

    w = jnp.dot(f.astype(BF16), mf_ref[...], preferred_element_type=F32)
    wri_ref[0, 0] = w[:, :F_WIDTH].astype(wri_ref.dtype)
    wri_ref[0, 1] = w[:, F_WIDTH:].astype(wri_ref.dtype)


def _premix_call(x, sh, sc, lw, cos_t, sin_t, tm):
    B, N, D = x.shape
    row = lambda b, i: (b, i, 0)
    vec = lambda b, i: (b, 0, 0)
    in_specs = [
        pl.BlockSpec((1, tm, D), row),
        pl.BlockSpec((1, 1, D), vec),
        pl.BlockSpec((1, 1, D), vec),
        _full((1, D)),
        _full((D, IN_COLS)),
        _full((1, Q_LORA)),
        _full((Q_LORA, HSLOTS)),
        _full((Q_LORA, HSLOTS)),
        _full((1, KV_LORA)),
        _full((KV_LORA, HSLOTS)),
        _full((KV_LORA, MLA_WIDTH)),
        _full((2 * SLOT, 2 * SLOT)),
        pl.BlockSpec((tm, SLOT), lambda b, i: (i, 0)),
        pl.BlockSpec((tm, SLOT), lambda b, i: (i, 0)),
        _full((1, HSLOTS)),
        _full((1, HSLOTS)),
        _full((1, HSLOTS)),
        _full((1, HSLOTS)),
        _full((F_WIDTH, 2 * F_WIDTH)),
    ]
    out_specs = [
        pl.BlockSpec((1, tm, HSLOTS), row),
        pl.BlockSpec((1, tm, HSLOTS), row),
        pl.BlockSpec((1, tm, MLA_WIDTH), row),
        pl.BlockSpec((1, 2, tm, F_WIDTH), lambda b, i: (b, 0, i, 0)),
    ]
    out_shape = [
        jax.ShapeDtypeStruct((B, N, HSLOTS), BF16),
        jax.ShapeDtypeStruct((B, N, HSLOTS), BF16),
        jax.ShapeDtypeStruct((B, N, MLA_WIDTH), BF16),
        jax.ShapeDtypeStruct((B, 2, N, F_WIDTH), BF16),
    ]
    return pl.pallas_call(
        _premix_kernel,
        grid=(B, N // tm),
        in_specs=in_specs,
        out_specs=out_specs,
        out_shape=out_shape,
        compiler_params=_cparams(("parallel", "parallel")),
        name="premix_proj",
    )(x, sh, sc, lw["gmix"], lw["win"], lw["qa"], lw["wq"], lw["wqs"], lw["kva"], lw["wk"], lw["wv"],
      lw["esum"], cos_t, sin_t, lw["gq"], lw["gqs"], lw["gk"], lw["gks"], lw["mf"])


def _attn_kernel(q_ref, k_ref, v_ref, ot_ref, vt_scr, qt_scr, *, n_kt, bounded, lookahead, vpu_denominator):
    tq = q_ref.shape[1]
    tk = ATTN_TK

    @pl.when(pl.program_id(2) == 0)
    def _():
        odd = (pl.program_id(1) % 2) == 1
        ones = jnp.ones((VT_ROWS - V_DIM, tk), BF16)
        for t in range(n_kt):
            vt_pair = v_ref[0, t * tk:(t + 1) * tk, :].T
            vt_scr[t, :V_DIM, :] = jnp.where(odd, vt_pair[V_DIM:], vt_pair[:V_DIM])
            vt_scr[t, V_DIM:, :] = ones

    qt_scr[...] = q_ref[0].T
    qt = qt_scr[...]

    def scores(t):
        off = t * tk if isinstance(t, int) else pl.multiple_of(t * tk, tk)
        return jnp.dot(k_ref[0, pl.ds(off, tk), :], qt, preferred_element_type=F32)

    def pv(t, p):
        return jnp.dot(vt_scr[t], p.astype(BF16), preferred_element_type=F32)

    if bounded and vpu_denominator:
        acc = jnp.zeros((V_DIM, tq), F32)
        den = jnp.zeros((1, tq), F32)
        ahead = [scores(t) for t in range(min(lookahead, n_kt))]
        for t in range(n_kt):
            if t + lookahead < n_kt:
                ahead.append(scores(t + lookahead))
            p = jnp.exp2(ahead.pop(0))
            den = den + jnp.sum(p, axis=0, keepdims=True)
            acc = acc + jnp.dot(vt_scr[t, :V_DIM, :], p.astype(BF16), preferred_element_type=F32)
        ot_ref[0] = (acc / den).astype(ot_ref.dtype)
        return
    if bounded:
        acc = jnp.zeros((VT_ROWS, tq), F32)
        ahead = [scores(t) for t in range(min(lookahead, n_kt))]
        for t in range(n_kt):
            if t + lookahead < n_kt:
                ahead.append(scores(t + lookahead))
            acc = acc + pv(t, jnp.exp2(ahead.pop(0)))
    else:
        unroll = _attn_unroll(n_kt)

        def body(g, carry):
            m, acc, s = carry
            for u in range(unroll):
                t = g * unroll + u
                s_next = scores(jnp.minimum(t + 1, n_kt - 1))
                m_new = jnp.maximum(m, jnp.max(s, axis=0, keepdims=True))
                acc = jnp.exp2(m - m_new) * acc + pv(t, jnp.exp2(s - m_new))
                m, s = m_new, s_next
            return m, acc, s

        init = (jnp.full((1, tq), -1e30, F32), jnp.zeros((VT_ROWS, tq), F32), scores(0))
        _, acc, _ = lax.fori_loop(0, n_kt // unroll, body, init)
    ot_ref[0] = (acc[:V_DIM] / acc[V_DIM:V_DIM + 1]).astype(ot_ref.dtype)


ATTN_TK = 256
VT_ROWS = V_DIM + 16
ATTN_LAYER_CFG = ((512, 2, True), (512, 2, True), (512, 2, False), (512, 2, False))
ATTN_SCORE_BOUND = 60.0


def _attn_unroll(n_kt):
    for u in (13, 9, 8, 5, 4, 3, 2):
        if n_kt % u == 0:
            return u
    return 1


def _attn_call(q, k, v, tq, bounded, lookahead, vpu_denominator):
    B, Nq, _ = q.shape
    Nk = k.shape[1]
    n_kt = Nk // ATTN_TK
    kern = functools.partial(_attn_kernel, n_kt=n_kt, bounded=bounded, lookahead=lookahead,
                             vpu_denominator=vpu_denominator)
    return pl.pallas_call(
        kern,
        grid=(B, N_HEADS, Nq // tq),
        in_specs=[
            pl.BlockSpec((1, tq, SLOT), lambda b, h, i: (b, i, h)),
            pl.BlockSpec((1, Nk, SLOT), lambda b, h, i: (b, 0, h)),
            pl.BlockSpec((1, Nk, 2 * V_DIM), lambda b, h, i: (b, 0, h // 2)),
        ],
        out_specs=pl.BlockSpec((1, V_DIM, tq), lambda b, h, i: (b, h, i)),
        out_shape=jax.ShapeDtypeStruct((B, MLA_WIDTH, Nq), BF16),
        scratch_shapes=[pltpu.VMEM((n_kt, VT_ROWS, ATTN_TK), BF16), pltpu.VMEM((SLOT, tq), BF16)],
        compiler_params=_cparams(("arbitrary", "arbitrary", "arbitrary")),
        name="mla_attention_bounded" if bounded else "mla_attention_online",
    )(q, k, v)


def _attention(q, k, v, tq, scores_bounded, lookahead=2, vpu_denominator=False):
    return lax.cond(scores_bounded,
                    lambda a, b, c: _attn_call(a, b, c, tq, True, lookahead, vpu_denominator),
                    lambda a, b, c: _attn_call(a, b, c, tq, False, 1, False),
                    q, k, v)


def _scores_bounded(q_gain, k_gain):
    bound = math.sqrt(QK_DIM) * math.log2(math.e) * jnp.max(jnp.abs(q_gain)) * jnp.max(jnp.abs(k_gain))
    return bound <= ATTN_SCORE_BOUND


def _fft1_kernel(f1_ref, tr_ref, ti_ref, w_ref, y_ref, *, n1, n2_per):
    y = jnp.dot(f1_ref[...], w_ref[0], preferred_element_type=F32)
    yr = y[:n1]
    yi = y[n1:]
    for t in range(n2_per):
        tr = tr_ref[:, t * LANES:(t + 1) * LANES]
        ti = ti_ref[:, t * LANES:(t + 1) * LANES]
        tr4 = jnp.concatenate([tr] * F_GROUPS, axis=-1)
        ti4 = jnp.concatenate([ti] * F_GROUPS, axis=-1)
        a = yr[:, t * F_WIDTH:(t + 1) * F_WIDTH]
        b = yi[:, t * F_WIDTH:(t + 1) * F_WIDTH]
        y_ref[0, :n1, t * F_WIDTH:(t + 1) * F_WIDTH] = (a * tr4 - b * ti4).astype(y_ref.dtype)
        y_ref[0, n1:, t * F_WIDTH:(t + 1) * F_WIDTH] = (a * ti4 + b * tr4).astype(y_ref.dtype)


def _fft2_kernel(c2_ref, s2_ref, y_ref, o_ref, *, kc):
    for i in range(kc):
        re = jnp.dot(c2_ref[...], y_ref[0, 0, i], preferred_element_type=F32)
        im = jnp.dot(s2_ref[...], y_ref[0, 1, i], preferred_element_type=F32)
        o_ref[0, :, i * F_WIDTH:(i + 1) * F_WIDTH] = (re + im).astype(o_ref.dtype)


def _fft_consts(n):
    n2 = LANES
    n1 = n // n2
    a1 = 2.0 * np.pi * np.outer(np.arange(n1), np.arange(n1)) / n1
    c1, s1 = np.cos(a1), np.sin(a1)
    f1 = np.block([[c1, s1], [-s1, c1]]).astype(np.float32)
    at = 2.0 * np.pi * np.outer(np.arange(n1), np.arange(n2)) / n
    tr = np.repeat(np.cos(at), LANES, axis=1).astype(np.float32)
    ti = np.repeat(-np.sin(at), LANES, axis=1).astype(np.float32)
    a2 = 2.0 * np.pi * np.outer(np.arange(n2), np.arange(n2)) / n2
    scale = 1.0 / math.sqrt(n)
    c2 = (np.cos(a2) * scale).astype(np.float32)
    s2 = (np.sin(a2) * scale).astype(np.float32)
    return n1, n2, f1, tr, ti, c2, s2


def _fft_call(wri):
    B, _, N, _ = wri.shape
    n1, n2, f1, tr, ti, c2, s2 = _fft_consts(N)
    cols = n2 * F_WIDTH
    n2_per = 4
    tc = n2_per * F_WIDTH
    y = pl.pallas_call(
        functools.partial(_fft1_kernel, n1=n1, n2_per=n2_per),
        grid=(B, cols // tc),
        in_specs=[
            _full((2 * n1, 2 * n1)),
            pl.BlockSpec((n1, n2_per * LANES), lambda b, j: (0, j)),
            pl.BlockSpec((n1, n2_per * LANES), lambda b, j: (0, j)),
            pl.BlockSpec((1, 2 * n1, tc), lambda b, j: (b, 0, j)),
        ],
        out_specs=pl.BlockSpec((1, 2 * n1, tc), lambda b, j: (b, 0, j)),
        out_shape=jax.ShapeDtypeStruct((B, 2 * n1, cols), BF16),
        compiler_params=_cparams(("parallel", "parallel")),
        name="fft_stage1",
    )(jnp.asarray(f1, BF16), jnp.asarray(tr), jnp.asarray(ti), wri.reshape(B, 2 * n1, cols))
    kc = min(8, n1)
    out = pl.pallas_call(
        functools.partial(_fft2_kernel, kc=kc),
        grid=(B, n1 // kc),
        in_specs=[
            _full((n2, n2)),
            _full((n2, n2)),
            pl.BlockSpec((1, 2, kc, n2, F_WIDTH), lambda b, j: (b, 0, j, 0, 0)),
        ],
        out_specs=pl.BlockSpec((1, n2, kc * F_WIDTH), lambda b, j: (b, 0, j)),
        out_shape=jax.ShapeDtypeStruct((B, n2, n1 * F_WIDTH), BF16),
        compiler_params=_cparams(("parallel", "parallel")),
        name="fft_stage2",
    )(jnp.asarray(c2, BF16), jnp.asarray(s2, BF16), y.reshape(B, 2, n1, n2, F_WIDTH))
    return out.reshape(B, N, F_WIDTH)


def _dft_small_kernel(c_ref, s_ref, w_ref, o_ref):
    re = jnp.dot(c_ref[...], w_ref[0, 0], preferred_element_type=F32)
    im = jnp.dot(s_ref[...], w_ref[0, 1], preferred_element_type=F32)
    o_ref[0] = (re + im).astype(o_ref.dtype)


def _dft_small_call(wri):
    B, _, N, _ = wri.shape
    a = 2.0 * np.pi * np.outer(np.arange(N), np.arange(N)) / N
    c = (np.cos(a) / math.sqrt(N)).astype(np.float32)
    s = (np.sin(a) / math.sqrt(N)).astype(np.float32)
    return pl.pallas_call(
        _dft_small_kernel,
        grid=(B,),
        in_specs=[_full((N, N)), _full((N, N)), pl.BlockSpec((1, 2, N, F_WIDTH), lambda b: (b, 0, 0, 0))],
        out_specs=pl.BlockSpec((1, N, F_WIDTH), lambda b: (b, 0, 0)),
        out_shape=jax.ShapeDtypeStruct((B, N, F_WIDTH), BF16),
        compiler_params=_cparams(("parallel",)),
        name="dft_small",
    )(jnp.asarray(c, BF16), jnp.asarray(s, BF16), wri)


def _router_rows(logits, bias):
    t = logits.shape[1]
    s = jax.nn.sigmoid(logits)
    b = s + bias
    srow = [s[e:e + 1, :] for e in range(N_EXPERTS)]
    brow = [b[e:e + 1, :] for e in range(N_EXPERTS)]
    gscore = []
    for g in range(N_EGROUPS):
        b0, b1, b2, b3 = brow[EPG * g:EPG * g + EPG]
        m1, n1 = jnp.maximum(b0, b1), jnp.minimum(b0, b1)
        m2, n2 = jnp.maximum(b2, b3), jnp.minimum(b2, b3)
        top1 = jnp.maximum(m1, m2)
        top2 = jnp.maximum(jnp.minimum(m1, m2), jnp.maximum(n1, n2))
        gscore.append(top1 + top2)
    best = gscore[0]
    gi = jnp.zeros((1, t), jnp.int32)
    for g in range(1, N_EGROUPS):
        take = gscore[g] > best
        gi = jnp.where(take, g, gi)
        best = jnp.where(take, gscore[g], best)
    ib, isc = [], []
    for i in range(EPG):
        vb, vs = brow[i], srow[i]
        for g in range(1, N_EGROUPS):
            sel = gi == g
            vb = jnp.where(sel, brow[EPG * g + i], vb)
            vs = jnp.where(sel, srow[EPG * g + i], vs)
        ib.append(vb)
        isc.append(vs)
    w = []
    for i in range(EPG):
        rank = jnp.zeros((1, t), jnp.int32)
        for j in range(EPG):
            if j == i:
                continue
            ahead = (ib[j] > ib[i]) | ((ib[j] == ib[i]) & (j < i))
            rank = rank + ahead.astype(jnp.int32)
        w.append(jnp.where(rank < 2, isc[i], 0.0))
    tot = (w[0] + w[1]) + (w[2] + w[3])
    inv = 1.0 / tot
    ridx = lax.broadcasted_iota(jnp.int32, (ROUTE_ROWS, t), 0)
    out = jnp.where(ridx == 2 * EPG, gi.astype(F32), 0.0)
    for i in range(EPG):
        out = jnp.where((ridx == i) | (ridx == EPG + i), w[i] * inv, out)
    return out


def _postmix_kernel(ot_ref, z_ref, x_ref, g1_ref, sh_ref, sc_ref, gn_ref, wo_ref, rw_ref, rb_ref,
                    xo_ref, f_ref, wd_ref):
    mix = lax.dot_general(ot_ref[0], wo_ref[:MLA_WIDTH, :], (((0,), (0,)), ((), ())), preferred_element_type=F32)
    mix = mix + jnp.dot(z_ref[0], wo_ref[MLA_WIDTH:, :], preferred_element_type=F32)
    xn = x_ref[0] + g1_ref[0] * mix
    xo_ref[0] = xn
    f = _rms(xn) * gn_ref[...] * (1.0 + sc_ref[0]) + sh_ref[0]
    f_ref[0] = f.astype(f_ref.dtype)
    logits = lax.dot_general(rw_ref[...], f, (((1,), (1,)), ((), ())), preferred_element_type=F32,
                             precision=HIGHEST)
    wd_ref[0] = _router_rows(logits, rb_ref[...])


def _postmix_call(o, z, x, g1, sh, sc, lw, rwt, rb, tm):
    B, N, D = x.shape
    row = lambda b, i: (b, i, 0)
    vec = lambda b, i: (b, 0, 0)
    return pl.pallas_call(
        _postmix_kernel,
        grid=(B, N // tm),
        in_specs=[
            pl.BlockSpec((1, MLA_WIDTH, tm), lambda b, i: (b, 0, i)),
            pl.BlockSpec((1, tm, F_WIDTH), row),
            pl.BlockSpec((1, tm, D), row),
            pl.BlockSpec((1, 1, D), vec),
            pl.BlockSpec((1, 1, D), vec),
            pl.BlockSpec((1, 1, D), vec),
            _full((1, D)),
            _full((D, D)),
            _full((N_EXPERTS, D)),
            _full((N_EXPERTS, 1)),
        ],
        out_specs=[
            pl.BlockSpec((1, tm, D), row),
            pl.BlockSpec((1, tm, D), row),
            pl.BlockSpec((1, ROUTE_ROWS, tm), lambda b, i: (b, 0, i)),
        ],
        out_shape=[
            jax.ShapeDtypeStruct((B, N, D), F32),
            jax.ShapeDtypeStruct((B, N, D), BF16),
            jax.ShapeDtypeStruct((B, ROUTE_ROWS, N), F32),
        ],
        compiler_params=_cparams(("parallel", "parallel")),
        name="postmix_router",
    )(o, z, x, g1, sh, sc, lw["gffn"], lw["wout"], rwt, rb)


def _expert_mlp(xb, wexp, g, wgate_ref, wup_ref, wdown_ref):
    hg = jnp.dot(xb, wgate_ref[g], preferred_element_type=F32)
    hu = jnp.dot(xb, wup_ref[g], preferred_element_type=F32)
    h = (hg * jax.nn.sigmoid(hg)) * hu * wexp
    return jnp.dot(h.astype(BF16), wdown_ref[g], preferred_element_type=F32)


def _moe_kernel(nch_ref, f_ref, rt_ref, x_ref, g2_ref, tri_ref, ex_ref, wgate_ref, wup_ref, wdown_ref, xo_ref, y_scr):
    tm = f_ref.shape[0]
    tile = pl.program_id(0)
    f = f_ref[...]
    rt = rt_ref[0]
    gid = rt[2 * EPG:2 * EPG + 1, :]
    r16 = lax.broadcasted_iota(jnp.int32, (ROUTE_ROWS, tm), 0)
    hi = rt.astype(BF16).astype(F32)
    whl = jnp.where(r16 < EPG, hi, jnp.where(r16 < 2 * EPG, rt - hi, 0.0)).astype(BF16)
    onehot = (r16.astype(F32) == gid).astype(BF16)
    before = jnp.dot(onehot, tri_ref[...], preferred_element_type=F32)
    rank = jnp.sum(onehot.astype(F32) * before, axis=0, keepdims=True)
    slot0 = lax.broadcasted_iota(jnp.int32, (MOE_CHUNK, tm), 0)
    y_scr[...] = jnp.zeros(y_scr.shape, y_scr.dtype)
    for g in range(N_EGROUPS):
        rank_g = jnp.where(gid == g, rank, -1.0)

        def chunk(c, carry, g=g, rank_g=rank_g):
            sel = ((slot0 + c * MOE_CHUNK).astype(F32) == rank_g).astype(BF16)
            xg = jnp.dot(sel, f, preferred_element_type=F32).astype(BF16)
            wr = lax.dot_general(sel, whl, (((1,), (1,)), ((), ())), preferred_element_type=F32).astype(BF16)
            wexp = jnp.dot(wr, ex_ref[...], preferred_element_type=F32)
            yg = _expert_mlp(xg, wexp, g, wgate_ref, wup_ref, wdown_ref)
            y_scr[...] += lax.dot_general(sel, yg.astype(BF16), (((0,), (0,)), ((), ())),
                                          preferred_element_type=F32)
            return carry

        lax.fori_loop(0, nch_ref[tile * N_EGROUPS + g], chunk, 0)
    xo_ref[...] = x_ref[...] + g2_ref[0] * y_scr[...]


MOE_CHUNK = 128


def _moe_call(f, rt, x, g2, lw, tm):
    B, N, D = x.shape
    GW = EPG * D_EXPERT
    tpb = N // tm
    n_tiles = B * tpb
    gid = rt[:, 2 * EPG, :].reshape(n_tiles, tm)
    counts = jnp.stack([jnp.sum(gid == g, axis=1) for g in range(N_EGROUPS)], axis=1)
    nch = ((counts + MOE_CHUNK - 1) // MOE_CHUNK).astype(jnp.int32).reshape(n_tiles * N_EGROUPS)
    tri = jnp.asarray(np.triu(np.ones((tm, tm), np.float32), 1), BF16)
    resident = dict(pipeline_mode=pl.Buffered(1))
    grid_spec = pltpu.PrefetchScalarGridSpec(
        num_scalar_prefetch=1,
        grid=(n_tiles,),
        in_specs=[
            pl.BlockSpec((tm, D), lambda i, o: (i, 0)),
            pl.BlockSpec((1, ROUTE_ROWS, tm), lambda i, o: (i // tpb, 0, i % tpb)),
            pl.BlockSpec((tm, D), lambda i, o: (i, 0)),
            pl.BlockSpec((1, 1, D), lambda i, o: (i // tpb, 0, 0)),
            pl.BlockSpec((tm, tm), lambda i, o: (0, 0), **resident),
            pl.BlockSpec((ROUTE_ROWS, GW), lambda i, o: (0, 0), **resident),
            pl.BlockSpec((N_EGROUPS, D, GW), lambda i, o: (0, 0, 0), **resident),
            pl.BlockSpec((N_EGROUPS, D, GW), lambda i, o: (0, 0, 0), **resident),
            pl.BlockSpec((N_EGROUPS, GW, D), lambda i, o: (0, 0, 0), **resident),
        ],
        out_specs=pl.BlockSpec((tm, D), lambda i, o: (i, 0)),
        scratch_shapes=[pltpu.VMEM((tm, D), F32)],
    )
    out = pl.pallas_call(
        _moe_kernel,
        grid_spec=grid_spec,
        out_shape=jax.ShapeDtypeStruct((B * N, D), F32),
        compiler_params=_cparams(("arbitrary",)),
        name="moe_experts",
    )(nch, f.reshape(B * N, D), rt, x.reshape(B * N, D), g2, tri, lw["expand"], lw["wgate"], lw["wup"], lw["wdown"])
    return out.reshape(B, N, D)


def _rope_partner():
    p = np.arange(SLOT)
    for j in range(ROPE_HALF):
        for base in (QK_NOPE, QK_NOPE + ROPE_AXIS):
            p[base + j] = base + ROPE_HALF + j
            p[base + ROPE_HALF + j] = base + j
    return p


def _rope_tables(n):
    f32 = np.float32
    rows = n // GRID_W
    row = np.repeat(np.arange(rows, dtype=f32), GRID_W)
    col = np.tile(np.arange(GRID_W, dtype=f32), rows)
    freqs = np.power(f32(ROPE_BASE), -np.arange(ROPE_HALF, dtype=f32) / f32(ROPE_HALF)).astype(f32)
    ar, ac = (row[:, None] * freqs).astype(f32), (col[:, None] * freqs).astype(f32)
    one = np.ones((n, QK_NOPE), f32)
    zero = np.zeros((n, QK_NOPE), f32)
    pad1 = np.ones((n, SLOT - QK_DIM), f32)
    pad0 = np.zeros((n, SLOT - QK_DIM), f32)
    cos_t = np.concatenate([one, np.cos(ar), np.cos(ar), np.cos(ac), np.cos(ac), pad1], axis=1).astype(f32)
    sin_t = np.concatenate([zero, -np.sin(ar), np.sin(ar), -np.sin(ac), np.sin(ac), pad0], axis=1).astype(f32)
    return jnp.asarray(cos_t), jnp.asarray(sin_t)


def _slot_cols(w, width):
    kdim = w.shape[0]
    w = w.reshape(kdim, N_HEADS, width)
    return jnp.pad(w, ((0, 0), (0, 0), (0, SLOT - width))).reshape(kdim, HSLOTS)


def _layer_weights(l, p, mf_all):
    partner = _rope_partner()
    w_in = p["w_in"][l]
    kr_cols = w_in[:, Q_LORA + KV_LORA:Q_LORA + KV_LORA + QK_ROPE]
    kr_slot = jnp.pad(kr_cols, ((0, 0), (QK_NOPE, SLOT - QK_DIM)))
    kr_swap = jnp.where((np.arange(SLOT) >= QK_NOPE) & (np.arange(SLOT) < QK_DIM), kr_slot[:, partner], 0.0)
    win = jnp.concatenate([w_in[:, :Q_LORA + KV_LORA], kr_slot, kr_swap, w_in[:, Q_LORA + KV_LORA + QK_ROPE:]], axis=1)

    rope_lane = (np.arange(SLOT) >= QK_NOPE) & (np.arange(SLOT) < QK_DIM)
    wq = _slot_cols(p["w_q_b"][l], QK_DIM)
    wq3 = wq.reshape(Q_LORA, N_HEADS, SLOT)
    wqs = jnp.where(rope_lane, wq3[:, :, partner], 0.0).reshape(Q_LORA, HSLOTS)

    wkv = p["w_kv_b"][l].reshape(KV_LORA, N_HEADS, QK_NOPE + V_DIM)
    wk = jnp.pad(wkv[:, :, :QK_NOPE], ((0, 0), (0, 0), (0, SLOT - QK_NOPE))).reshape(KV_LORA, HSLOTS)
    wv = wkv[:, :, QK_NOPE:].reshape(KV_LORA, MLA_WIDTH)

    def gains(g, scale):
        gp = jnp.pad(g, (0, SLOT - QK_DIM))
        gs = jnp.where(rope_lane, gp[partner], 0.0)
        return (jnp.tile(gp, N_HEADS) * scale)[None, :], (jnp.tile(gs, N_HEADS) * scale)[None, :]

    gq, gqs = gains(p["q_norm"][l], QK_DIM ** -0.5 * math.log2(math.e))
    gk, gks = gains(p["k_norm"][l], 1.0)

    def group_cols(w):
        return w.reshape(N_EGROUPS, EPG, D_MODEL, D_EXPERT).transpose(0, 2, 1, 3).reshape(
            N_EGROUPS, D_MODEL, EPG * D_EXPERT).astype(BF16)

    return {
        "gmix": p["norm_mix"][l][None, :],
        "gffn": p["norm_ffn"][l][None, :],
        "win": win.astype(BF16),
        "qa": p["q_a_norm"][l][None, :],
        "wq": wq.astype(BF16),
        "wqs": wqs.astype(BF16),
        "kva": p["kv_a_norm"][l][None, :],
        "wk": wk.astype(BF16),
        "wv": wv.astype(BF16),
        "gq": gq, "gqs": gqs, "gk": gk, "gks": gks,
        "mf": mf_all[l],
        "wout": p["w_out"][l].astype(BF16),
        "wgate": group_cols(p["w_gate"][l]),
        "wup": group_cols(p["w_up"][l]),
        "wdown": p["w_down"][l].reshape(N_EGROUPS, EPG * D_EXPERT, D_MODEL).astype(BF16),
    }


def _const_tables():
    lane_head = np.arange(2 * SLOT) // SLOT
    esum = (lane_head[:, None] == lane_head[None, :]).astype(np.float32)
    ex = np.zeros((ROUTE_ROWS, EPG * D_EXPERT), np.float32)
    for i in range(EPG):
        ex[i, i * D_EXPERT:(i + 1) * D_EXPERT] = 1.0
        ex[EPG + i, i * D_EXPERT:(i + 1) * D_EXPERT] = 1.0
    return jnp.asarray(esum, BF16), jnp.asarray(ex, BF16)


def _mod6(mod_l, rows, B):
    m = mod_l[jnp.asarray(rows)]
    return [m[:, i * D_MODEL:(i + 1) * D_MODEL][:, None, :] for i in range(6)]


def kernel(x, c, ctx, c_ctx, w_ada, b_ada, norm_mix, norm_ffn, w_in, q_a_norm, w_q_b, kv_a_norm, w_kv_b, q_norm,
           k_norm, w_fourier, w_out, router_w, router_bias, w_gate, w_up, w_down):
    B, S, D = x.shape
    n_ctx = ctx.shape[1]
    depth = w_ada.shape[0]
    p = dict(w_in=w_in, q_a_norm=q_a_norm, w_q_b=w_q_b, kv_a_norm=kv_a_norm, w_kv_b=w_kv_b, q_norm=q_norm,
             k_norm=k_norm, norm_mix=norm_mix, norm_ffn=norm_ffn, w_out=w_out, w_gate=w_gate, w_up=w_up,
             w_down=w_down)

    c8 = jnp.concatenate([c, c_ctx[None, :], jnp.zeros((8 - B - 1, D), F32)], axis=0)
    mod = _ada_call(c8, w_ada, b_ada)
    mf_all = _fmat_call(w_fourier)
    esum, expand = _const_tables()
    cos_l, sin_l = _rope_tables(S)
    cos_c = jnp.ones((n_ctx, SLOT), F32)
    sin_c = jnp.zeros((n_ctx, SLOT), F32)
    rwt = router_w.T
    rb = router_bias[:, None]

    tm_lat = 512
    tm_ctx = min(256, n_ctx)
    for l in range(depth):
        last = l == depth - 1
        lw = _layer_weights(l, p, mf_all)
        lw["esum"] = esum
        lw["expand"] = expand
        sh1, sc1, g1, sh2, sc2, g2 = _mod6(mod[l], list(range(B)), B)
        csh1, csc1, cg1, csh2, csc2, cg2 = _mod6(mod[l], [B] * B, B)

        q_c, k_c, v_c, wri_c = _premix_call(ctx, csh1, csc1, lw, cos_c, sin_c, tm_ctx)
        q_l, k_l, v_l, wri_l = _premix_call(x, sh1, sc1, lw, cos_l, sin_l, tm_lat)
        k_all = jnp.concatenate([k_c, k_l], axis=1)
        v_all = jnp.concatenate([v_c, v_l], axis=1)
        bounded = _scores_bounded(q_norm[l], k_norm[l])
        tq_l, la_l, vden_l = ATTN_LAYER_CFG[l % len(ATTN_LAYER_CFG)]
        o_l = _attention(q_l, k_all, v_all, tq_l, bounded, la_l, vden_l)
        z_l = _fft_call(wri_l)
        x_mid, f_l, wd_l = _postmix_call(o_l, z_l, x, g1, sh2, sc2, lw, rwt, rb, tm_lat)
        x = _moe_call(f_l, wd_l, x_mid, g2, lw, tm_lat)
        if not last:
            o_c = _attention(q_c, k_c, v_c, tm_ctx, bounded)
            z_c = _dft_small_call(wri_c)
            c_mid, f_c, wd_c = _postmix_call(o_c, z_c, ctx, cg1, csh2, csc2, lw, rwt, rb, tm_ctx)
            ctx = _moe_call(f_c, wd_c, c_mid, cg2, lw, tm_ctx)
    return x
```

```python
import functools
import math

import numpy as np
import jax
import jax.numpy as jnp
from jax import lax
from jax.experimental import pallas as pl
from jax.experimental.pallas import tpu as pltpu

F32 = jnp.float32
BF16 = jnp.bfloat16
HIGHEST = lax.Precision.HIGHEST

D_MODEL = 1024
N_HEADS = 8
QK_NOPE = 64
QK_ROPE = 32
QK_DIM = QK_NOPE + QK_ROPE
V_DIM = 64
Q_LORA = 256
KV_LORA = 128
ROPE_AXIS = QK_ROPE // 2
ROPE_HALF = ROPE_AXIS // 2
ROPE_BASE = 10000.0
GRID_W = 64
F_GROUPS = 4
F_GDIM = 128
F_WIDTH = F_GROUPS * F_GDIM
MLA_WIDTH = N_HEADS * V_DIM
N_EXPERTS = 16
N_EGROUPS = 4
EPG = N_EXPERTS // N_EGROUPS
D_EXPERT = 256
EPS = 1e-6
ROUTE_ROWS = 16

LANES = 128
SLOT = LANES
HSLOTS = N_HEADS * SLOT
VMEM_LIMIT = 56 * 1024 * 1024

C_CQ = 0
C_CKV = C_CQ + Q_LORA
C_KR = C_CKV + KV_LORA
C_KRS = C_KR + SLOT
C_F = C_KRS + SLOT
IN_COLS = C_F + F_WIDTH


def _cparams(sem):
    return pltpu.CompilerParams(dimension_semantics=sem, vmem_limit_bytes=VMEM_LIMIT)


def _full(shape):
    n = len(shape)
    return pl.BlockSpec(shape, lambda *_: (0,) * n)


def _ada_kernel(c_ref, w_ref, b_ref, o_ref):
    c = c_ref[...]
    s = c * jax.nn.sigmoid(c)
    o_ref[0] = jnp.dot(s, w_ref[0], preferred_element_type=F32, precision=HIGHEST) + b_ref[0]


def _ada_call(c8, w_ada, b_ada):
    L, D, W = w_ada.shape
    tn = 1536
    return pl.pallas_call(
        _ada_kernel,
        grid=(L, W // tn),
        in_specs=[
            pl.BlockSpec((8, D), lambda l, j: (0, 0)),
            pl.BlockSpec((1, D, tn), lambda l, j: (l, 0, j)),
            pl.BlockSpec((1, 1, tn), lambda l, j: (l, 0, j)),
        ],
        out_specs=pl.BlockSpec((1, 8, tn), lambda l, j: (l, 0, j)),
        out_shape=jax.ShapeDtypeStruct((L, 8, W), F32),
        compiler_params=_cparams(("parallel", "parallel")),
        name="adaln_mod",
    )(c8, w_ada, b_ada.reshape(L, 1, W))


def _fmat_kernel(cc_ref, sc_ref, wf_ref, o_ref):
    o_ref[...] = jnp.zeros(o_ref.shape, o_ref.dtype)
    for g in range(F_GROUPS):
        wf = wf_ref[0, g]
        mr = jnp.dot(cc_ref[...], wf, preferred_element_type=F32, precision=HIGHEST)
        mi = jnp.dot(sc_ref[...], wf, preferred_element_type=F32, precision=HIGHEST)
        r0 = g * F_GDIM
        o_ref[0, r0:r0 + F_GDIM, r0:r0 + F_GDIM] = mr.astype(o_ref.dtype)
        o_ref[0, r0:r0 + F_GDIM, F_WIDTH + r0:F_WIDTH + r0 + F_GDIM] = mi.astype(o_ref.dtype)


def _fmat_call(w_fourier):
    L = w_fourier.shape[0]
    c = np.arange(F_GDIM)
    ang = 2.0 * np.pi * np.outer(c, c) / F_GDIM
    cc = jnp.asarray((np.cos(ang) / math.sqrt(F_GDIM)).astype(np.float32))
    sc = jnp.asarray((-np.sin(ang) / math.sqrt(F_GDIM)).astype(np.float32))
    return pl.pallas_call(
        _fmat_kernel,
        grid=(L,),
        in_specs=[
            _full((F_GDIM, F_GDIM)),
            _full((F_GDIM, F_GDIM)),
            pl.BlockSpec((1, F_GROUPS, F_GDIM, F_GDIM), lambda l: (l, 0, 0, 0)),
        ],
        out_specs=pl.BlockSpec((1, F_WIDTH, 2 * F_WIDTH), lambda l: (l, 0, 0)),
        out_shape=jax.ShapeDtypeStruct((L, F_WIDTH, 2 * F_WIDTH), BF16),
        compiler_params=_cparams(("parallel",)),
        name="fourier_chan_mats",
    )(cc, sc, w_fourier)


def _rms(x):
    return x * lax.rsqrt(jnp.mean(x * x, axis=-1, keepdims=True) + EPS)


def _tile8(a):
    return jnp.concatenate([a] * N_HEADS, axis=-1)


def _head_sums(x2, e2):
    w = e2.shape[0]
    xb = x2.astype(BF16)
    return jnp.concatenate(
        [jnp.dot(xb[:, c * w:(c + 1) * w], e2, preferred_element_type=F32) for c in range(HSLOTS // w)], axis=-1)


def _premix_kernel(x_ref, sh_ref, sc_ref, gn_ref, win_ref, qa_ref, wq_ref, wqs_ref, kva_ref, wk_ref, wv_ref,
                   e_ref, cos_ref, sin_ref, gq_ref, gqs_ref, gk_ref, gks_ref, mf_ref,
                   q_ref, k_ref, v_ref, wri_ref):
    x = x_ref[0]
    h = _rms(x) * gn_ref[...] * (1.0 + sc_ref[0]) + sh_ref[0]
    proj = jnp.dot(h.astype(BF16), win_ref[...], preferred_element_type=F32)
    cq = proj[:, C_CQ:C_CKV]
    ckv = proj[:, C_CKV:C_KR]
    kr = proj[:, C_KR:C_KRS]
    krs = proj[:, C_KRS:C_F]
    f = proj[:, C_F:IN_COLS]

    cos8 = _tile8(cos_ref[...])
    sin8 = _tile8(sin_ref[...])
    inv_d = 1.0 / QK_DIM

    cqn = (_rms(cq) * qa_ref[...]).astype(BF16)
    qr = jnp.dot(cqn, wq_ref[...], preferred_element_type=F32)
    qs = jnp.dot(cqn, wqs_ref[...], preferred_element_type=F32)
    ssq = _head_sums(qr * qr, e_ref[...])
    rq = lax.rsqrt(ssq * inv_d + EPS)
    q = rq * (qr * (gq_ref[...] * cos8) + qs * (gqs_ref[...] * sin8))
    q_ref[0] = q.astype(q_ref.dtype)

    ckvn = (_rms(ckv) * kva_ref[...]).astype(BF16)
    kraw = jnp.dot(ckvn, wk_ref[...], preferred_element_type=F32) + _tile8(kr)
    ssk = _head_sums(kraw * kraw, e_ref[...])
    rk = lax.rsqrt(ssk * inv_d + EPS)
    k = rk * (kraw * (gk_ref[...] * cos8) + _tile8(krs) * (gks_ref[...] * sin8))
    k_ref[0] = k.astype(k_ref.dtype)
    v_ref[0] = jnp.dot(ckvn, wv_ref[...], preferred_element_type=F32).astype(v_ref.dtype)

    w = jnp.dot(f.astype(BF16), mf_ref[...], preferred_element_type=F32)
    wri_ref[0, 0] = w[:, :F_WIDTH].astype(wri_ref.dtype)
    wri_ref[0, 1] = w[:, F_WIDTH:].astype(wri_ref.dtype)


def _premix_call(x, sh, sc, lw, cos_t, sin_t, tm):
    B, N, D = x.shape
    row = lambda b, i: (b, i, 0)
    vec = lambda b, i: (b, 0, 0)
    in_specs = [
        pl.BlockSpec((1, tm, D), row),
        pl.BlockSpec((1, 1, D), vec),
        pl.BlockSpec((1, 1, D), vec),
        _full((1, D)),
        _full((D, IN_COLS)),
        _full((1, Q_LORA)),
        _full((Q_LORA, HSLOTS)),
        _full((Q_LORA, HSLOTS)),
        _full((1, KV_LORA)),
        _full((KV_LORA, HSLOTS)),
        _full((KV_LORA, MLA_WIDTH)),
        _full((2 * SLOT, 2 * SLOT)),
        pl.BlockSpec((tm, SLOT), lambda b, i: (i, 0)),
        pl.BlockSpec((tm, SLOT), lambda b, i: (i, 0)),
        _full((1, HSLOTS)),
        _full((1, HSLOTS)),
        _full((1, HSLOTS)),
        _full((1, HSLOTS)),
        _full((F_WIDTH, 2 * F_WIDTH)),
    ]
    out_specs = [
        pl.BlockSpec((1, tm, HSLOTS), row),
        pl.BlockSpec((1, tm, HSLOTS), row),
        pl.BlockSpec((1, tm, MLA_WIDTH), row),
        pl.BlockSpec((1, 2, tm, F_WIDTH), lambda b, i: (b, 0, i, 0)),
    ]
    out_shape = [
        jax.ShapeDtypeStruct((B, N, HSLOTS), BF16),
        jax.ShapeDtypeStruct((B, N, HSLOTS), BF16),
        jax.ShapeDtypeStruct((B, N, MLA_WIDTH), BF16),
        jax.ShapeDtypeStruct((B, 2, N, F_WIDTH), BF16),
    ]
    return pl.pallas_call(
        _premix_kernel,
        grid=(B, N // tm),
        in_specs=in_specs,
        out_specs=out_specs,
        out_shape=out_shape,
        compiler_params=_cparams(("parallel", "parallel")),
        name="premix_proj",
    )(x, sh, sc, lw["gmix"], lw["win"], lw["qa"], lw["wq"], lw["wqs"], lw["kva"], lw["wk"], lw["wv"],
      lw["esum"], cos_t, sin_t, lw["gq"], lw["gqs"], lw["gk"], lw["gks"], lw["mf"])


def _attn_kernel(q_ref, k_ref, v_ref, ot_ref, vt_scr, qt_scr, *, n_kt, bounded):
    tq = q_ref.shape[1]
    tk = ATTN_TK

    @pl.when(pl.program_id(2) == 0)
    def _():
        odd = (pl.program_id(1) % 2) == 1
        for t in range(n_kt):
            vt_pair = v_ref[0, t * tk:(t + 1) * tk, :].T
            vt_scr[t] = jnp.where(odd, vt_pair[V_DIM:], vt_pair[:V_DIM])

    qt_scr[...] = q_ref[0].T
    qt = qt_scr[...]

    def scores(t):
        off = t * tk if isinstance(t, int) else pl.multiple_of(t * tk, tk)
        return jnp.dot(k_ref[0, pl.ds(off, tk), :], qt, preferred_element_type=F32)

    def pv(t, p):
        return jnp.dot(vt_scr[t], p.astype(BF16), preferred_element_type=F32)

    if bounded:
        acc = jnp.zeros((V_DIM, tq), F32)
        den = jnp.zeros((1, tq), F32)
        ahead = [scores(t) for t in range(min(ATTN_LOOKAHEAD, n_kt))]
        for t in range(n_kt):
            if t + ATTN_LOOKAHEAD < n_kt:
                ahead.append(scores(t + ATTN_LOOKAHEAD))
            p = jnp.exp2(ahead.pop(0))
            den = den + jnp.sum(p, axis=0, keepdims=True)
            acc = acc + pv(t, p)
    else:
        unroll = _attn_unroll(n_kt)

        def body(g, carry):
            m, den, acc, s = carry
            for u in range(unroll):
                t = g * unroll + u
                s_next = scores(jnp.minimum(t + 1, n_kt - 1))
                m_new = jnp.maximum(m, jnp.max(s, axis=0, keepdims=True))
                alpha = jnp.exp2(m - m_new)
                p = jnp.exp2(s - m_new)
                den = alpha * den + jnp.sum(p, axis=0, keepdims=True)
                acc = alpha * acc + pv(t, p)
                m, s = m_new, s_next
            return m, den, acc, s

        init = (jnp.full((1, tq), -1e30, F32), jnp.zeros((1, tq), F32), jnp.zeros((V_DIM, tq), F32), scores(0))
        _, den, acc, _ = lax.fori_loop(0, n_kt // unroll, body, init)
    ot_ref[0] = (acc / den).astype(ot_ref.dtype)


ATTN_TK = 256
ATTN_TQ = 512
ATTN_LOOKAHEAD = 2
ATTN_SCORE_BOUND = 60.0


def _attn_unroll(n_kt):
    for u in (13, 9, 8, 5, 4, 3, 2):
        if n_kt % u == 0:
            return u
    return 1


def _attn_call(q, k, v, tq, bounded):
    B, Nq, _ = q.shape
    Nk = k.shape[1]
    n_kt = Nk // ATTN_TK
    kern = functools.partial(_attn_kernel, n_kt=n_kt, bounded=bounded)
    return pl.pallas_call(
        kern,
        grid=(B, N_HEADS, Nq // tq),
        in_specs=[
            pl.BlockSpec((1, tq, SLOT), lambda b, h, i: (b, i, h)),
            pl.BlockSpec((1, Nk, SLOT), lambda b, h, i: (b, 0, h)),
            pl.BlockSpec((1, Nk, 2 * V_DIM), lambda b, h, i: (b, 0, h // 2)),
        ],
        out_specs=pl.BlockSpec((1, V_DIM, tq), lambda b, h, i: (b, h, i)),
        out_shape=jax.ShapeDtypeStruct((B, MLA_WIDTH, Nq), BF16),
        scratch_shapes=[pltpu.VMEM((n_kt, V_DIM, ATTN_TK), BF16), pltpu.VMEM((SLOT, tq), BF16)],
        compiler_params=_cparams(("arbitrary", "arbitrary", "arbitrary")),
        name="mla_attention_bounded" if bounded else "mla_attention_online",
    )(q, k, v)


def _attention(q, k, v, tq, scores_bounded):
    return lax.cond(scores_bounded,
                    lambda a, b, c: _attn_call(a, b, c, tq, True),
                    lambda a, b, c: _attn_call(a, b, c, tq, False),
                    q, k, v)


def _scores_bounded(q_gain, k_gain):
    bound = math.sqrt(QK_DIM) * math.log2(math.e) * jnp.max(jnp.abs(q_gain)) * jnp.max(jnp.abs(k_gain))
    return bound <= ATTN_SCORE_BOUND


def _fft1_kernel(f1_ref, tr_ref, ti_ref, w_ref, y_ref, *, n1, n2_per):
    y = jnp.dot(f1_ref[...], w_ref[0], preferred_element_type=F32)
    yr = y[:n1]
    yi = y[n1:]
    for t in range(n2_per):
        tr = tr_ref[:, t * LANES:(t + 1) * LANES]
        ti = ti_ref[:, t * LANES:(t + 1) * LANES]
        tr4 = jnp.concatenate([tr] * F_GROUPS, axis=-1)
        ti4 = jnp.concatenate([ti] * F_GROUPS, axis=-1)
        a = yr[:, t * F_WIDTH:(t + 1) * F_WIDTH]
        b = yi[:, t * F_WIDTH:(t + 1) * F_WIDTH]
        y_ref[0, :n1, t * F_WIDTH:(t + 1) * F_WIDTH] = (a * tr4 - b * ti4).astype(y_ref.dtype)
        y_ref[0, n1:, t * F_WIDTH:(t + 1) * F_WIDTH] = (a * ti4 + b * tr4).astype(y_ref.dtype)


def _fft2_kernel(c2_ref, s2_ref, y_ref, o_ref, *, kc):
    for i in range(kc):
        re = jnp.dot(c2_ref[...], y_ref[0, 0, i], preferred_element_type=F32)
        im = jnp.dot(s2_ref[...], y_ref[0, 1, i], preferred_element_type=F32)
        o_ref[0, :, i * F_WIDTH:(i + 1) * F_WIDTH] = (re + im).astype(o_ref.dtype)


def _fft_consts(n):
    n2 = LANES
    n1 = n // n2
    a1 = 2.0 * np.pi * np.outer(np.arange(n1), np.arange(n1)) / n1
    c1, s1 = np.cos(a1), np.sin(a1)
    f1 = np.block([[c1, s1], [-s1, c1]]).astype(np.float32)
    at = 2.0 * np.pi * np.outer(np.arange(n1), np.arange(n2)) / n
    tr = np.repeat(np.cos(at), LANES, axis=1).astype(np.float32)
    ti = np.repeat(-np.sin(at), LANES, axis=1).astype(np.float32)
    a2 = 2.0 * np.pi * np.outer(np.arange(n2), np.arange(n2)) / n2
    scale = 1.0 / math.sqrt(n)
    c2 = (np.cos(a2) * scale).astype(np.float32)
    s2 = (np.sin(a2) * scale).astype(np.float32)
    return n1, n2, f1, tr, ti, c2, s2


def _fft_call(wri):
    B, _, N, _ = wri.shape
    n1, n2, f1, tr, ti, c2, s2 = _fft_consts(N)
    cols = n2 * F_WIDTH
    n2_per = 4
    tc = n2_per * F_WIDTH
    y = pl.pallas_call(
        functools.partial(_fft1_kernel, n1=n1, n2_per=n2_per),
        grid=(B, cols // tc),
        in_specs=[
            _full((2 * n1, 2 * n1)),
            pl.BlockSpec((n1, n2_per * LANES), lambda b, j: (0, j)),
            pl.BlockSpec((n1, n2_per * LANES), lambda b, j: (0, j)),
            pl.BlockSpec((1, 2 * n1, tc), lambda b, j: (b, 0, j)),
        ],
        out_specs=pl.BlockSpec((1, 2 * n1, tc), lambda b, j: (b, 0, j)),
        out_shape=jax.ShapeDtypeStruct((B, 2 * n1, cols), BF16),
        compiler_params=_cparams(("parallel", "parallel")),
        name="fft_stage1",
    )(jnp.asarray(f1, BF16), jnp.asarray(tr), jnp.asarray(ti), wri.reshape(B, 2 * n1, cols))
    kc = min(8, n1)
    out = pl.pallas_call(
        functools.partial(_fft2_kernel, kc=kc),
        grid=(B, n1 // kc),
        in_specs=[
            _full((n2, n2)),
            _full((n2, n2)),
            pl.BlockSpec((1, 2, kc, n2, F_WIDTH), lambda b, j: (b, 0, j, 0, 0)),
        ],
        out_specs=pl.BlockSpec((1, n2, kc * F_WIDTH), lambda b, j: (b, 0, j)),
        out_shape=jax.ShapeDtypeStruct((B, n2, n1 * F_WIDTH), BF16),
        compiler_params=_cparams(("parallel", "parallel")),
        name="fft_stage2",
    )(jnp.asarray(c2, BF16), jnp.asarray(s2, BF16), y.reshape(B, 2, n1, n2, F_WIDTH))
    return out.reshape(B, N, F_WIDTH)


def _dft_small_kernel(c_ref, s_ref, w_ref, o_ref):
    re = jnp.dot(c_ref[...], w_ref[0, 0], preferred_element_type=F32)
    im = jnp.dot(s_ref[...], w_ref[0, 1], preferred_element_type=F32)
    o_ref[0] = (re + im).astype(o_ref.dtype)


def _dft_small_call(wri):
    B, _, N, _ = wri.shape
    a = 2.0 * np.pi * np.outer(np.arange(N), np.arange(N)) / N
    c = (np.cos(a) / math.sqrt(N)).astype(np.float32)
    s = (np.sin(a) / math.sqrt(N)).astype(np.float32)
    return pl.pallas_call(
        _dft_small_kernel,
        grid=(B,),
        in_specs=[_full((N, N)), _full((N, N)), pl.BlockSpec((1, 2, N, F_WIDTH), lambda b: (b, 0, 0, 0))],
        out_specs=pl.BlockSpec((1, N, F_WIDTH), lambda b: (b, 0, 0)),
        out_shape=jax.ShapeDtypeStruct((B, N, F_WIDTH), BF16),
        compiler_params=_cparams(("parallel",)),
        name="dft_small",
    )(jnp.asarray(c, BF16), jnp.asarray(s, BF16), wri)


def _router_rows(logits, bias):
    t = logits.shape[1]
    s = jax.nn.sigmoid(logits)
    b = s + bias
    srow = [s[e:e + 1, :] for e in range(N_EXPERTS)]
    brow = [b[e:e + 1, :] for e in range(N_EXPERTS)]
    gscore = []
    for g in range(N_EGROUPS):
        b0, b1, b2, b3 = brow[EPG * g:EPG * g + EPG]
        m1, n1 = jnp.maximum(b0, b1), jnp.minimum(b0, b1)
        m2, n2 = jnp.maximum(b2, b3), jnp.minimum(b2, b3)
        top1 = jnp.maximum(m1, m2)
        top2 = jnp.maximum(jnp.minimum(m1, m2), jnp.maximum(n1, n2))
        gscore.append(top1 + top2)
    best = gscore[0]
    gi = jnp.zeros((1, t), jnp.int32)
    for g in range(1, N_EGROUPS):
        take = gscore[g] > best
        gi = jnp.where(take, g, gi)
        best = jnp.where(take, gscore[g], best)
    ib, isc = [], []
    for i in range(EPG):
        vb, vs = brow[i], srow[i]
        for g in range(1, N_EGROUPS):
            sel = gi == g
            vb = jnp.where(sel, brow[EPG * g + i], vb)
            vs = jnp.where(sel, srow[EPG * g + i], vs)
        ib.append(vb)
        isc.append(vs)
    w = []
    for i in range(EPG):
        rank = jnp.zeros((1, t), jnp.int32)
        for j in range(EPG):
            if j == i:
                continue
            ahead = (ib[j] > ib[i]) | ((ib[j] == ib[i]) & (j < i))
            rank = rank + ahead.astype(jnp.int32)
        w.append(jnp.where(rank < 2, isc[i], 0.0))
    tot = (w[0] + w[1]) + (w[2] + w[3])
    inv = 1.0 / tot
    ridx = lax.broadcasted_iota(jnp.int32, (ROUTE_ROWS, t), 0)
    out = jnp.where(ridx == 2 * EPG, gi.astype(F32), 0.0)
    for i in range(EPG):
        out = jnp.where((ridx == i) | (ridx == EPG + i), w[i] * inv, out)
    return out


def _postmix_kernel(ot_ref, z_ref, x_ref, g1_ref, sh_ref, sc_ref, gn_ref, wo_ref, rw_ref, rb_ref,
                    xo_ref, f_ref, wd_ref):
    mix = lax.dot_general(ot_ref[0], wo_ref[:MLA_WIDTH, :], (((0,), (0,)), ((), ())), preferred_element_type=F32)
    mix = mix + jnp.dot(z_ref[0], wo_ref[MLA_WIDTH:, :], preferred_element_type=F32)
    xn = x_ref[0] + g1_ref[0] * mix
    xo_ref[0] = xn
    f = _rms(xn) * gn_ref[...] * (1.0 + sc_ref[0]) + sh_ref[0]
    f_ref[0] = f.astype(f_ref.dtype)
    logits = lax.dot_general(rw_ref[...], f, (((1,), (1,)), ((), ())), preferred_element_type=F32,
                             precision=HIGHEST)
    wd_ref[0] = _router_rows(logits, rb_ref[...])


def _postmix_call(o, z, x, g1, sh, sc, lw, rwt, rb, tm):
    B, N, D = x.shape
    row = lambda b, i: (b, i, 0)
    vec = lambda b, i: (b, 0, 0)
    return pl.pallas_call(
        _postmix_kernel,
        grid=(B, N // tm),
        in_specs=[
            pl.BlockSpec((1, MLA_WIDTH, tm), lambda b, i: (b, 0, i)),
            pl.BlockSpec((1, tm, F_WIDTH), row),
            pl.BlockSpec((1, tm, D), row),
            pl.BlockSpec((1, 1, D), vec),
            pl.BlockSpec((1, 1, D), vec),
            pl.BlockSpec((1, 1, D), vec),
            _full((1, D)),
            _full((D, D)),
            _full((N_EXPERTS, D)),
            _full((N_EXPERTS, 1)),
        ],
        out_specs=[
            pl.BlockSpec((1, tm, D), row),
            pl.BlockSpec((1, tm, D), row),
            pl.BlockSpec((1, ROUTE_ROWS, tm), lambda b, i: (b, 0, i)),
        ],
        out_shape=[
            jax.ShapeDtypeStruct((B, N, D), F32),
            jax.ShapeDtypeStruct((B, N, D), BF16),
            jax.ShapeDtypeStruct((B, ROUTE_ROWS, N), F32),
        ],
        compiler_params=_cparams(("parallel", "parallel")),
        name="postmix_router",
    )(o, z, x, g1, sh, sc, lw["gffn"], lw["wout"], rwt, rb)


def _expert_mlp(xb, wexp, g, wgate_ref, wup_ref, wdown_ref):
    hg = jnp.dot(xb, wgate_ref[g], preferred_element_type=F32)
    hu = jnp.dot(xb, wup_ref[g], preferred_element_type=F32)
    h = (hg * jax.nn.sigmoid(hg)) * hu * wexp
    return jnp.dot(h.astype(BF16), wdown_ref[g], preferred_element_type=F32)


def _moe_kernel(nch_ref, f_ref, rt_ref, x_ref, g2_ref, tri_ref, ex_ref, wgate_ref, wup_ref, wdown_ref, xo_ref, y_scr):
    tm = f_ref.shape[0]
    tile = pl.program_id(0)
    f = f_ref[...]
    rt = rt_ref[0]
    gid = rt[2 * EPG:2 * EPG + 1, :]
    r16 = lax.broadcasted_iota(jnp.int32, (ROUTE_ROWS, tm), 0)
    hi = rt.astype(BF16).astype(F32)
    whl = jnp.where(r16 < EPG, hi, jnp.where(r16 < 2 * EPG, rt - hi, 0.0)).astype(BF16)
    onehot = (r16.astype(F32) == gid).astype(BF16)
    before = jnp.dot(onehot, tri_ref[...], preferred_element_type=F32)
    rank = jnp.sum(onehot.astype(F32) * before, axis=0, keepdims=True)
    slot0 = lax.broadcasted_iota(jnp.int32, (MOE_CHUNK, tm), 0)
    y_scr[...] = jnp.zeros(y_scr.shape, y_scr.dtype)
    for g in range(N_EGROUPS):
        rank_g = jnp.where(gid == g, rank, -1.0)

        def chunk(c, carry, g=g, rank_g=rank_g):
            sel = ((slot0 + c * MOE_CHUNK).astype(F32) == rank_g).astype(BF16)
            xg = jnp.dot(sel, f, preferred_element_type=F32).astype(BF16)
            wr = lax.dot_general(sel, whl, (((1,), (1,)), ((), ())), preferred_element_type=F32).astype(BF16)
            wexp = jnp.dot(wr, ex_ref[...], preferred_element_type=F32)
            yg = _expert_mlp(xg, wexp, g, wgate_ref, wup_ref, wdown_ref)
            y_scr[...] += lax.dot_general(sel, yg.astype(BF16), (((0,), (0,)), ((), ())),
                                          preferred_element_type=F32)
            return carry

        lax.fori_loop(0, nch_ref[tile * N_EGROUPS + g], chunk, 0)
    xo_ref[...] = x_ref[...] + g2_ref[0] * y_scr[...]


MOE_CHUNK = 128


def _moe_call(f, rt, x, g2, lw, tm):
    B, N, D = x.shape
    GW = EPG * D_EXPERT
    tpb = N // tm
    n_tiles = B * tpb
    gid = rt[:, 2 * EPG, :].reshape(n_tiles, tm)
    counts = jnp.stack([jnp.sum(gid == g, axis=1) for g in range(N_EGROUPS)], axis=1)
    nch = ((counts + MOE_CHUNK - 1) // MOE_CHUNK).astype(jnp.int32).reshape(n_tiles * N_EGROUPS)
    tri = jnp.asarray(np.triu(np.ones((tm, tm), np.float32), 1), BF16)
    resident = dict(pipeline_mode=pl.Buffered(1))
    grid_spec = pltpu.PrefetchScalarGridSpec(
        num_scalar_prefetch=1,
        grid=(n_tiles,),
        in_specs=[
            pl.BlockSpec((tm, D), lambda i, o: (i, 0)),
            pl.BlockSpec((1, ROUTE_ROWS, tm), lambda i, o: (i // tpb, 0, i % tpb)),
            pl.BlockSpec((tm, D), lambda i, o: (i, 0)),
            pl.BlockSpec((1, 1, D), lambda i, o: (i // tpb, 0, 0)),
            pl.BlockSpec((tm, tm), lambda i, o: (0, 0), **resident),
            pl.BlockSpec((ROUTE_ROWS, GW), lambda i, o: (0, 0), **resident),
            pl.BlockSpec((N_EGROUPS, D, GW), lambda i, o: (0, 0, 0), **resident),
            pl.BlockSpec((N_EGROUPS, D, GW), lambda i, o: (0, 0, 0), **resident),
            pl.BlockSpec((N_EGROUPS, GW, D), lambda i, o: (0, 0, 0), **resident),
        ],
        out_specs=pl.BlockSpec((tm, D), lambda i, o: (i, 0)),
        scratch_shapes=[pltpu.VMEM((tm, D), F32)],
    )
    out = pl.pallas_call(
        _moe_kernel,
        grid_spec=grid_spec,
        out_shape=jax.ShapeDtypeStruct((B * N, D), F32),
        compiler_params=_cparams(("arbitrary",)),
        name="moe_experts",
    )(nch, f.reshape(B * N, D), rt, x.reshape(B * N, D), g2, tri, lw["expand"], lw["wgate"], lw["wup"], lw["wdown"])
    return out.reshape(B, N, D)


def _rope_partner():
    p = np.arange(SLOT)
    for j in range(ROPE_HALF):
        for base in (QK_NOPE, QK_NOPE + ROPE_AXIS):
            p[base + j] = base + ROPE_HALF + j
            p[base + ROPE_HALF + j] = base + j
    return p


def _rope_tables(n):
    f32 = np.float32
    rows = n // GRID_W
    row = np.repeat(np.arange(rows, dtype=f32), GRID_W)
    col = np.tile(np.arange(GRID_W, dtype=f32), rows)
    freqs = np.power(f32(ROPE_BASE), -np.arange(ROPE_HALF, dtype=f32) / f32(ROPE_HALF)).astype(f32)
    ar, ac = (row[:, None] * freqs).astype(f32), (col[:, None] * freqs).astype(f32)
    one = np.ones((n, QK_NOPE), f32)
    zero = np.zeros((n, QK_NOPE), f32)
    pad1 = np.ones((n, SLOT - QK_DIM), f32)
    pad0 = np.zeros((n, SLOT - QK_DIM), f32)
    cos_t = np.concatenate([one, np.cos(ar), np.cos(ar), np.cos(ac), np.cos(ac), pad1], axis=1).astype(f32)
    sin_t = np.concatenate([zero, -np.sin(ar), np.sin(ar), -np.sin(ac), np.sin(ac), pad0], axis=1).astype(f32)
    return jnp.asarray(cos_t), jnp.asarray(sin_t)


def _slot_cols(w, width):
    kdim = w.shape[0]
    w = w.reshape(kdim, N_HEADS, width)
    return jnp.pad(w, ((0, 0), (0, 0), (0, SLOT - width))).reshape(kdim, HSLOTS)


def _layer_weights(l, p, mf_all):
    partner = _rope_partner()
    w_in = p["w_in"][l]
    kr_cols = w_in[:, Q_LORA + KV_LORA:Q_LORA + KV_LORA + QK_ROPE]
    kr_slot = jnp.pad(kr_cols, ((0, 0), (QK_NOPE, SLOT - QK_DIM)))
    kr_swap = jnp.where((np.arange(SLOT) >= QK_NOPE) & (np.arange(SLOT) < QK_DIM), kr_slot[:, partner], 0.0)
    win = jnp.concatenate([w_in[:, :Q_LORA + KV_LORA], kr_slot, kr_swap, w_in[:, Q_LORA + KV_LORA + QK_ROPE:]], axis=1)

    rope_lane = (np.arange(SLOT) >= QK_NOPE) & (np.arange(SLOT) < QK_DIM)
    wq = _slot_cols(p["w_q_b"][l], QK_DIM)
    wq3 = wq.reshape(Q_LORA, N_HEADS, SLOT)
    wqs = jnp.where(rope_lane, wq3[:, :, partner], 0.0).reshape(Q_LORA, HSLOTS)

    wkv = p["w_kv_b"][l].reshape(KV_LORA, N_HEADS, QK_NOPE + V_DIM)
    wk = jnp.pad(wkv[:, :, :QK_NOPE], ((0, 0), (0, 0), (0, SLOT - QK_NOPE))).reshape(KV_LORA, HSLOTS)
    wv = wkv[:, :, QK_NOPE:].reshape(KV_LORA, MLA_WIDTH)

    def gains(g, scale):
        gp = jnp.pad(g, (0, SLOT - QK_DIM))
        gs = jnp.where(rope_lane, gp[partner], 0.0)
        return (jnp.tile(gp, N_HEADS) * scale)[None, :], (jnp.tile(gs, N_HEADS) * scale)[None, :]

    gq, gqs = gains(p["q_norm"][l], QK_DIM ** -0.5 * math.log2(math.e))
    gk, gks = gains(p["k_norm"][l], 1.0)

    def group_cols(w):
        return w.reshape(N_EGROUPS, EPG, D_MODEL, D_EXPERT).transpose(0, 2, 1, 3).reshape(
            N_EGROUPS, D_MODEL, EPG * D_EXPERT).astype(BF16)

    return {
        "gmix": p["norm_mix"][l][None, :],
        "gffn": p["norm_ffn"][l][None, :],
        "win": win.astype(BF16),
        "qa": p["q_a_norm"][l][None, :],
        "wq": wq.astype(BF16),
        "wqs": wqs.astype(BF16),
        "kva": p["kv_a_norm"][l][None, :],
        "wk": wk.astype(BF16),
        "wv": wv.astype(BF16),
        "gq": gq, "gqs": gqs, "gk": gk, "gks": gks,
        "mf": mf_all[l],
        "wout": p["w_out"][l].astype(BF16),
        "wgate": group_cols(p["w_gate"][l]),
        "wup": group_cols(p["w_up"][l]),
        "wdown": p["w_down"][l].reshape(N_EGROUPS, EPG * D_EXPERT, D_MODEL).astype(BF16),
    }


def _const_tables():
    lane_head = np.arange(2 * SLOT) // SLOT
    esum = (lane_head[:, None] == lane_head[None, :]).astype(np.float32)
    ex = np.zeros((ROUTE_ROWS, EPG * D_EXPERT), np.float32)
    for i in range(EPG):
        ex[i, i * D_EXPERT:(i + 1) * D_EXPERT] = 1.0
        ex[EPG + i, i * D_EXPERT:(i + 1) * D_EXPERT] = 1.0
    return jnp.asarray(esum, BF16), jnp.asarray(ex, BF16)


def _mod6(mod_l, rows, B):
    m = mod_l[jnp.asarray(rows)]
    return [m[:, i * D_MODEL:(i + 1) * D_MODEL][:, None, :] for i in range(6)]


def kernel(x, c, ctx, c_ctx, w_ada, b_ada, norm_mix, norm_ffn, w_in, q_a_norm, w_q_b, kv_a_norm, w_kv_b, q_norm,
           k_norm, w_fourier, w_out, router_w, router_bias, w_gate, w_up, w_down):
    B, S, D = x.shape
    n_ctx = ctx.shape[1]
    depth = w_ada.shape[0]
    p = dict(w_in=w_in, q_a_norm=q_a_norm, w_q_b=w_q_b, kv_a_norm=kv_a_norm, w_kv_b=w_kv_b, q_norm=q_norm,
             k_norm=k_norm, norm_mix=norm_mix, norm_ffn=norm_ffn, w_out=w_out, w_gate=w_gate, w_up=w_up,
             w_down=w_down)

    c8 = jnp.concatenate([c, c_ctx[None, :], jnp.zeros((8 - B - 1, D), F32)], axis=0)
    mod = _ada_call(c8, w_ada, b_ada)
    mf_all = _fmat_call(w_fourier)
    esum, expand = _const_tables()
    cos_l, sin_l = _rope_tables(S)
    cos_c = jnp.ones((n_ctx, SLOT), F32)
    sin_c = jnp.zeros((n_ctx, SLOT), F32)
    rwt = router_w.T
    rb = router_bias[:, None]

    tm_lat = 512
    tm_ctx = min(256, n_ctx)
    for l in range(depth):
        last = l == depth - 1
        lw = _layer_weights(l, p, mf_all)
        lw["esum"] = esum
        lw["expand"] = expand
        sh1, sc1, g1, sh2, sc2, g2 = _mod6(mod[l], list(range(B)), B)
        csh1, csc1, cg1, csh2, csc2, cg2 = _mod6(mod[l], [B] * B, B)

        q_c, k_c, v_c, wri_c = _premix_call(ctx, csh1, csc1, lw, cos_c, sin_c, tm_ctx)
        q_l, k_l, v_l, wri_l = _premix_call(x, sh1, sc1, lw, cos_l, sin_l, tm_lat)
        k_all = jnp.concatenate([k_c, k_l], axis=1)
        v_all = jnp.concatenate([v_c, v_l], axis=1)
        bounded = _scores_bounded(q_norm[l], k_norm[l])
        o_l = _attention(q_l, k_all, v_all, ATTN_TQ, bounded)
        z_l = _fft_call(wri_l)
        x_mid, f_l, wd_l = _postmix_call(o_l, z_l, x, g1, sh2, sc2, lw, rwt, rb, tm_lat)
        x = _moe_call(f_l, wd_l, x_mid, g2, lw, tm_lat)
        if not last:
            o_c = _attention(q_c, k_c, v_c, tm_ctx, bounded)
            z_c = _dft_small_call(wri_c)
            c_mid, f_c, wd_c = _postmix_call(o_c, z_c, ctx, cg1, csh2, csc2, lw, rwt, rb, tm_ctx)
            ctx = _moe_call(f_c, wd_c, c_mid, cg2, lw, tm_ctx)
    return x
```

```python
import functools
import math

import numpy as np
import jax
import jax.numpy as jnp
from jax import lax
from jax.experimental import pallas as pl
from jax.experimental.pallas import tpu as pltpu

F32 = jnp.float32
BF16 = jnp.bfloat16
HIGHEST = lax.Precision.HIGHEST

D_MODEL = 1024
N_HEADS = 8
QK_NOPE = 64
QK_ROPE = 32
QK_DIM = QK_NOPE + QK_ROPE
V_DIM = 64
Q_LORA = 256
KV_LORA = 128
ROPE_AXIS = QK_ROPE // 2
ROPE_HALF = ROPE_AXIS // 2
ROPE_BASE = 10000.0
GRID_W = 64
F_GROUPS = 4
F_GDIM = 128
F_WIDTH = F_GROUPS * F_GDIM
MLA_WIDTH = N_HEADS * V_DIM
N_EXPERTS = 16
N_EGROUPS = 4
EPG = N_EXPERTS // N_EGROUPS
D_EXPERT = 256
EPS = 1e-6
ROUTE_ROWS = 16

LANES = 128
SLOT = LANES
HSLOTS = N_HEADS * SLOT
VMEM_LIMIT = 56 * 1024 * 1024

C_CQ = 0
C_CKV = C_CQ + Q_LORA
C_KR = C_CKV + KV_LORA
C_KRS = C_KR + SLOT
C_F = C_KRS + SLOT
IN_COLS = C_F + F_WIDTH


def _cparams(sem):
    return pltpu.CompilerParams(dimension_semantics=sem, vmem_limit_bytes=VMEM_LIMIT)


def _full(shape):
    n = len(shape)
    return pl.BlockSpec(shape, lambda *_: (0,) * n)


def _ada_kernel(c_ref, w_ref, b_ref, o_ref):
    c = c_ref[...]
    s = c * jax.nn.sigmoid(c)
    o_ref[0] = jnp.dot(s, w_ref[0], preferred_element_type=F32, precision=HIGHEST) + b_ref[0]


def _ada_call(c8, w_ada, b_ada):
    L, D, W = w_ada.shape
    tn = 1536
    return pl.pallas_call(
        _ada_kernel,
        grid=(L, W // tn),
        in_specs=[
            pl.BlockSpec((8, D), lambda l, j: (0, 0)),
            pl.BlockSpec((1, D, tn), lambda l, j: (l, 0, j)),
            pl.BlockSpec((1, 1, tn), lambda l, j: (l, 0, j)),
        ],
        out_specs=pl.BlockSpec((1, 8, tn), lambda l, j: (l, 0, j)),
        out_shape=jax.ShapeDtypeStruct((L, 8, W), F32),
        compiler_params=_cparams(("parallel", "parallel")),
        name="adaln_mod",
    )(c8, w_ada, b_ada.reshape(L, 1, W))


def _fmat_kernel(cc_ref, sc_ref, wf_ref, o_ref):
    o_ref[...] = jnp.zeros(o_ref.shape, o_ref.dtype)
    for g in range(F_GROUPS):
        wf = wf_ref[0, g]
        mr = jnp.dot(cc_ref[...], wf, preferred_element_type=F32, precision=HIGHEST)
        mi = jnp.dot(sc_ref[...], wf, preferred_element_type=F32, precision=HIGHEST)
        r0 = g * F_GDIM
        o_ref[0, r0:r0 + F_GDIM, r0:r0 + F_GDIM] = mr.astype(o_ref.dtype)
        o_ref[0, r0:r0 + F_GDIM, F_WIDTH + r0:F_WIDTH + r0 + F_GDIM] = mi.astype(o_ref.dtype)


def _fmat_call(w_fourier):
    L = w_fourier.shape[0]
    c = np.arange(F_GDIM)
    ang = 2.0 * np.pi * np.outer(c, c) / F_GDIM
    cc = jnp.asarray((np.cos(ang) / math.sqrt(F_GDIM)).astype(np.float32))
    sc = jnp.asarray((-np.sin(ang) / math.sqrt(F_GDIM)).astype(np.float32))
    return pl.pallas_call(
        _fmat_kernel,
        grid=(L,),
        in_specs=[
            _full((F_GDIM, F_GDIM)),
            _full((F_GDIM, F_GDIM)),
            pl.BlockSpec((1, F_GROUPS, F_GDIM, F_GDIM), lambda l: (l, 0, 0, 0)),
        ],
        out_specs=pl.BlockSpec((1, F_WIDTH, 2 * F_WIDTH), lambda l: (l, 0, 0)),
        out_shape=jax.ShapeDtypeStruct((L, F_WIDTH, 2 * F_WIDTH), BF16),
        compiler_params=_cparams(("parallel",)),
        name="fourier_chan_mats",
    )(cc, sc, w_fourier)


def _rms(x):
    return x * lax.rsqrt(jnp.mean(x * x, axis=-1, keepdims=True) + EPS)


def _tile8(a):
    return jnp.concatenate([a] * N_HEADS, axis=-1)


def _head_sums(x2, e2):
    w = e2.shape[0]
    xb = x2.astype(BF16)
    return jnp.concatenate(
        [jnp.dot(xb[:, c * w:(c + 1) * w], e2, preferred_element_type=F32) for c in range(HSLOTS // w)], axis=-1)


def _premix_kernel(x_ref, sh_ref, sc_ref, gn_ref, win_ref, qa_ref, wq_ref, wqs_ref, kva_ref, wk_ref, wv_ref,
                   e_ref, cos_ref, sin_ref, gq_ref, gqs_ref, gk_ref, gks_ref, mf_ref,
                   q_ref, k_ref, v_ref, wri_ref):
    x = x_ref[0]
    h = _rms(x) * gn_ref[...] * (1.0 + sc_ref[0]) + sh_ref[0]
    proj = jnp.dot(h.astype(BF16), win_ref[...], preferred_element_type=F32)
    cq = proj[:, C_CQ:C_CKV]
    ckv = proj[:, C_CKV:C_KR]
    kr = proj[:, C_KR:C_KRS]
    krs = proj[:, C_KRS:C_F]
    f = proj[:, C_F:IN_COLS]

    cos8 = _tile8(cos_ref[...])
    sin8 = _tile8(sin_ref[...])
    inv_d = 1.0 / QK_DIM

    cqn = (_rms(cq) * qa_ref[...]).astype(BF16)
    qr = jnp.dot(cqn, wq_ref[...], preferred_element_type=F32)
    qs = jnp.dot(cqn, wqs_ref[...], preferred_element_type=F32)
    ssq = _head_sums(qr * qr, e_ref[...])
    rq = lax.rsqrt(ssq * inv_d + EPS)
    q = rq * (qr * (gq_ref[...] * cos8) + qs * (gqs_ref[...] * sin8))
    q_ref[0] = q.astype(q_ref.dtype)

    ckvn = (_rms(ckv) * kva_ref[...]).astype(BF16)
    kraw = jnp.dot(ckvn, wk_ref[...], preferred_element_type=F32) + _tile8(kr)
    ssk = _head_sums(kraw * kraw, e_ref[...])
    rk = lax.rsqrt(ssk * inv_d + EPS)
    k = rk * (kraw * (gk_ref[...] * cos8) + _tile8(krs) * (gks_ref[...] * sin8))
    k_ref[0] = k.astype(k_ref.dtype)
    v_ref[0] = jnp.dot(ckvn, wv_ref[...], preferred_element_type=F32).astype(v_ref.dtype)

    w = jnp.dot(f.astype(BF16), mf_ref[...], preferred_element_type=F32)
    wri_ref[0, 0] = w[:, :F_WIDTH].astype(wri_ref.dtype)
    wri_ref[0, 1] = w[:, F_WIDTH:].astype(wri_ref.dtype)


def _premix_call(x, sh, sc, lw, cos_t, sin_t, tm):
    B, N, D = x.shape
    row = lambda b, i: (b, i, 0)
    vec = lambda b, i: (b, 0, 0)
    in_specs = [
        pl.BlockSpec((1, tm, D), row),
        pl.BlockSpec((1, 1, D), vec),
        pl.BlockSpec((1, 1, D), vec),
        _full((1, D)),
        _full((D, IN_COLS)),
        _full((1, Q_LORA)),
        _full((Q_LORA, HSLOTS)),
        _full((Q_LORA, HSLOTS)),
        _full((1, KV_LORA)),
        _full((KV_LORA, HSLOTS)),
        _full((KV_LORA, MLA_WIDTH)),
        _full((2 * SLOT, 2 * SLOT)),
        pl.BlockSpec((tm, SLOT), lambda b, i: (i, 0)),
        pl.BlockSpec((tm, SLOT), lambda b, i: (i, 0)),
        _full((1, HSLOTS)),
        _full((1, HSLOTS)),
        _full((1, HSLOTS)),
        _full((1, HSLOTS)),
        _full((F_WIDTH, 2 * F_WIDTH)),
    ]
    out_specs = [
        pl.BlockSpec((1, tm, HSLOTS), row),
        pl.BlockSpec((1, tm, HSLOTS), row),
        pl.BlockSpec((1, tm, MLA_WIDTH), row),
        pl.BlockSpec((1, 2, tm, F_WIDTH), lambda b, i: (b, 0, i, 0)),
    ]
    out_shape = [
        jax.ShapeDtypeStruct((B, N, HSLOTS), BF16),
        jax.ShapeDtypeStruct((B, N, HSLOTS), BF16),
        jax.ShapeDtypeStruct((B, N, MLA_WIDTH), BF16),
        jax.ShapeDtypeStruct((B, 2, N, F_WIDTH), BF16),
    ]
    return pl.pallas_call(
        _premix_kernel,
        grid=(B, N // tm),
        in_specs=in_specs,
        out_specs=out_specs,
        out_shape=out_shape,
        compiler_params=_cparams(("parallel", "parallel")),
        name="premix_proj",
    )(x, sh, sc, lw["gmix"], lw["win"], lw["qa"], lw["wq"], lw["wqs"], lw["kva"], lw["wk"], lw["wv"],
      lw["esum"], cos_t, sin_t, lw["gq"], lw["gqs"], lw["gk"], lw["gks"], lw["mf"])


def _attn_kernel(q_ref, k_ref, v_ref, ot_ref, vt_scr, qt_scr, *, n_kt, bounded):
    tq = q_ref.shape[1]
    tk = ATTN_TK

    @pl.when(pl.program_id(2) == 0)
    def _():
        odd = (pl.program_id(1) % 2) == 1
        for t in range(n_kt):
            vt_pair = v_ref[0, t * tk:(t + 1) * tk, :].T
            vt_scr[t] = jnp.where(odd, vt_pair[V_DIM:], vt_pair[:V_DIM])

    qt_scr[...] = q_ref[0].T
    qt = qt_scr[...]

    def scores(t):
        off = t * tk if isinstance(t, int) else pl.multiple_of(t * tk, tk)
        return jnp.dot(k_ref[0, pl.ds(off, tk), :], qt, preferred_element_type=F32)

    def pv(t, p):
        return jnp.dot(vt_scr[t], p.astype(BF16), preferred_element_type=F32)

    if bounded:
        acc = jnp.zeros((V_DIM, tq), F32)
        den = jnp.zeros((1, tq), F32)
        ahead = [scores(t) for t in range(min(ATTN_LOOKAHEAD, n_kt))]
        for t in range(n_kt):
            if t + ATTN_LOOKAHEAD < n_kt:
                ahead.append(scores(t + ATTN_LOOKAHEAD))
            p = jnp.exp2(ahead.pop(0))
            den = den + jnp.sum(p, axis=0, keepdims=True)
            acc = acc + pv(t, p)
    else:
        unroll = _attn_unroll(n_kt)

        def body(g, carry):
            m, den, acc, s = carry
            for u in range(unroll):
                t = g * unroll + u
                s_next = scores(jnp.minimum(t + 1, n_kt - 1))
                m_new = jnp.maximum(m, jnp.max(s, axis=0, keepdims=True))
                alpha = jnp.exp2(m - m_new)
                p = jnp.exp2(s - m_new)
                den = alpha * den + jnp.sum(p, axis=0, keepdims=True)
                acc = alpha * acc + pv(t, p)
                m, s = m_new, s_next
            return m, den, acc, s

        init = (jnp.full((1, tq), -1e30, F32), jnp.zeros((1, tq), F32), jnp.zeros((V_DIM, tq), F32), scores(0))
        _, den, acc, _ = lax.fori_loop(0, n_kt // unroll, body, init)
    ot_ref[0] = (acc / den).astype(ot_ref.dtype)


ATTN_TK = 256
ATTN_TQ = 512
ATTN_LOOKAHEAD = 2
ATTN_SCORE_BOUND = 60.0


def _attn_unroll(n_kt):
    for u in (13, 9, 8, 5, 4, 3, 2):
        if n_kt % u == 0:
            return u
    return 1


def _attn_call(q, k, v, tq, bounded):
    B, Nq, _ = q.shape
    Nk = k.shape[1]
    n_kt = Nk // ATTN_TK
    kern = functools.partial(_attn_kernel, n_kt=n_kt, bounded=bounded)
    return pl.pallas_call(
        kern,
        grid=(B, N_HEADS, Nq // tq),
        in_specs=[
            pl.BlockSpec((1, tq, SLOT), lambda b, h, i: (b, i, h)),
            pl.BlockSpec((1, Nk, SLOT), lambda b, h, i: (b, 0, h)),
            pl.BlockSpec((1, Nk, 2 * V_DIM), lambda b, h, i: (b, 0, h // 2)),
        ],
        out_specs=pl.BlockSpec((1, V_DIM, tq), lambda b, h, i: (b, h, i)),
        out_shape=jax.ShapeDtypeStruct((B, MLA_WIDTH, Nq), BF16),
        scratch_shapes=[pltpu.VMEM((n_kt, V_DIM, ATTN_TK), BF16), pltpu.VMEM((SLOT, tq), BF16)],
        compiler_params=_cparams(("arbitrary", "arbitrary", "arbitrary")),
        name="mla_attention_bounded" if bounded else "mla_attention_online",
    )(q, k, v)


def _attention(q, k, v, tq, scores_bounded):
    return lax.cond(scores_bounded,
                    lambda a, b, c: _attn_call(a, b, c, tq, True),
                    lambda a, b, c: _attn_call(a, b, c, tq, False),
                    q, k, v)


def _scores_bounded(q_gain, k_gain):
    bound = math.sqrt(QK_DIM) * math.log2(math.e) * jnp.max(jnp.abs(q_gain)) * jnp.max(jnp.abs(k_gain))
    return bound <= ATTN_SCORE_BOUND


def _fft1_kernel(f1_ref, tr_ref, ti_ref, w_ref, y_ref, *, n1, n2_per):
    y = jnp.dot(f1_ref[...], w_ref[0], preferred_element_type=F32)
    yr = y[:n1]
    yi = y[n1:]
    for t in range(n2_per):
        tr = tr_ref[:, t * LANES:(t + 1) * LANES]
        ti = ti_ref[:, t * LANES:(t + 1) * LANES]
        tr4 = jnp.concatenate([tr] * F_GROUPS, axis=-1)
        ti4 = jnp.concatenate([ti] * F_GROUPS, axis=-1)
        a = yr[:, t * F_WIDTH:(t + 1) * F_WIDTH]
        b = yi[:, t * F_WIDTH:(t + 1) * F_WIDTH]
        y_ref[0, :n1, t * F_WIDTH:(t + 1) * F_WIDTH] = (a * tr4 - b * ti4).astype(y_ref.dtype)
        y_ref[0, n1:, t * F_WIDTH:(t + 1) * F_WIDTH] = (a * ti4 + b * tr4).astype(y_ref.dtype)


def _fft2_kernel(c2_ref, s2_ref, y_ref, o_ref, *, kc):
    for i in range(kc):
        re = jnp.dot(c2_ref[...], y_ref[0, 0, i], preferred_element_type=F32)
        im = jnp.dot(s2_ref[...], y_ref[0, 1, i], preferred_element_type=F32)
        o_ref[0, :, i * F_WIDTH:(i + 1) * F_WIDTH] = (re + im).astype(o_ref.dtype)


def _fft_consts(n):
    n2 = LANES
    n1 = n // n2
    a1 = 2.0 * np.pi * np.outer(np.arange(n1), np.arange(n1)) / n1
    c1, s1 = np.cos(a1), np.sin(a1)
    f1 = np.block([[c1, s1], [-s1, c1]]).astype(np.float32)
    at = 2.0 * np.pi * np.outer(np.arange(n1), np.arange(n2)) / n
    tr = np.repeat(np.cos(at), LANES, axis=1).astype(np.float32)
    ti = np.repeat(-np.sin(at), LANES, axis=1).astype(np.float32)
    a2 = 2.0 * np.pi * np.outer(np.arange(n2), np.arange(n2)) / n2
    scale = 1.0 / math.sqrt(n)
    c2 = (np.cos(a2) * scale).astype(np.float32)
    s2 = (np.sin(a2) * scale).astype(np.float32)
    return n1, n2, f1, tr, ti, c2, s2


def _fft_call(wri):
    B, _, N, _ = wri.shape
    n1, n2, f1, tr, ti, c2, s2 = _fft_consts(N)
    cols = n2 * F_WIDTH
    n2_per = 8
    tc = n2_per * F_WIDTH
    y = pl.pallas_call(
        functools.partial(_fft1_kernel, n1=n1, n2_per=n2_per),
        grid=(B, cols // tc),
        in_specs=[
            _full((2 * n1, 2 * n1)),
            pl.BlockSpec((n1, n2_per * LANES), lambda b, j: (0, j)),
            pl.BlockSpec((n1, n2_per * LANES), lambda b, j: (0, j)),
            pl.BlockSpec((1, 2 * n1, tc), lambda b, j: (b, 0, j)),
        ],
        out_specs=pl.BlockSpec((1, 2 * n1, tc), lambda b, j: (b, 0, j)),
        out_shape=jax.ShapeDtypeStruct((B, 2 * n1, cols), BF16),
        compiler_params=_cparams(("parallel", "parallel")),
        name="fft_stage1",
    )(jnp.asarray(f1, BF16), jnp.asarray(tr), jnp.asarray(ti), wri.reshape(B, 2 * n1, cols))
    kc = min(8, n1)
    out = pl.pallas_call(
        functools.partial(_fft2_kernel, kc=kc),
        grid=(B, n1 // kc),
        in_specs=[
            _full((n2, n2)),
            _full((n2, n2)),
            pl.BlockSpec((1, 2, kc, n2, F_WIDTH), lambda b, j: (b, 0, j, 0, 0)),
        ],
        out_specs=pl.BlockSpec((1, n2, kc * F_WIDTH), lambda b, j: (b, 0, j)),
        out_shape=jax.ShapeDtypeStruct((B, n2, n1 * F_WIDTH), BF16),
        compiler_params=_cparams(("parallel", "parallel")),
        name="fft_stage2",
    )(jnp.asarray(c2, BF16), jnp.asarray(s2, BF16), y.reshape(B, 2, n1, n2, F_WIDTH))
    return out.reshape(B, N, F_WIDTH)


def _dft_small_kernel(c_ref, s_ref, w_ref, o_ref):
    re = jnp.dot(c_ref[...], w_ref[0, 0], preferred_element_type=F32)
    im = jnp.dot(s_ref[...], w_ref[0, 1], preferred_element_type=F32)
    o_ref[0] = (re + im).astype(o_ref.dtype)


def _dft_small_call(wri):
    B, _, N, _ = wri.shape
    a = 2.0 * np.pi * np.outer(np.arange(N), np.arange(N)) / N
    c = (np.cos(a) / math.sqrt(N)).astype(np.float32)
    s = (np.sin(a) / math.sqrt(N)).astype(np.float32)
    return pl.pallas_call(
        _dft_small_kernel,
        grid=(B,),
        in_specs=[_full((N, N)), _full((N, N)), pl.BlockSpec((1, 2, N, F_WIDTH), lambda b: (b, 0, 0, 0))],
        out_specs=pl.BlockSpec((1, N, F_WIDTH), lambda b: (b, 0, 0)),
        out_shape=jax.ShapeDtypeStruct((B, N, F_WIDTH), BF16),
        compiler_params=_cparams(("parallel",)),
        name="dft_small",
    )(jnp.asarray(c, BF16), jnp.asarray(s, BF16), wri)


def _router_rows(logits, bias):
    t = logits.shape[1]
    s = jax.nn.sigmoid(logits)
    b = s + bias
    srow = [s[e:e + 1, :] for e in range(N_EXPERTS)]
    brow = [b[e:e + 1, :] for e in range(N_EXPERTS)]
    gscore = []
    for g in range(N_EGROUPS):
        b0, b1, b2, b3 = brow[EPG * g:EPG * g + EPG]
        m1, n1 = jnp.maximum(b0, b1), jnp.minimum(b0, b1)
        m2, n2 = jnp.maximum(b2, b3), jnp.minimum(b2, b3)
        top1 = jnp.maximum(m1, m2)
        top2 = jnp.maximum(jnp.minimum(m1, m2), jnp.maximum(n1, n2))
        gscore.append(top1 + top2)
    best = gscore[0]
    gi = jnp.zeros((1, t), jnp.int32)
    for g in range(1, N_EGROUPS):
        take = gscore[g] > best
        gi = jnp.where(take, g, gi)
        best = jnp.where(take, gscore[g], best)
    ib, isc = [], []
    for i in range(EPG):
        vb, vs = brow[i], srow[i]
        for g in range(1, N_EGROUPS):
            sel = gi == g
            vb = jnp.where(sel, brow[EPG * g + i], vb)
            vs = jnp.where(sel, srow[EPG * g + i], vs)
        ib.append(vb)
        isc.append(vs)
    w = []
    for i in range(EPG):
        rank = jnp.zeros((1, t), jnp.int32)
        for j in range(EPG):
            if j == i:
                continue
            ahead = (ib[j] > ib[i]) | ((ib[j] == ib[i]) & (j < i))
            rank = rank + ahead.astype(jnp.int32)
        w.append(jnp.where(rank < 2, isc[i], 0.0))
    tot = (w[0] + w[1]) + (w[2] + w[3])
    inv = 1.0 / tot
    ridx = lax.broadcasted_iota(jnp.int32, (ROUTE_ROWS, t), 0)
    out = jnp.where(ridx == 2 * EPG, gi.astype(F32), 0.0)
    for i in range(EPG):
        out = jnp.where((ridx == i) | (ridx == EPG + i), w[i] * inv, out)
    return out


def _postmix_kernel(ot_ref, z_ref, x_ref, g1_ref, sh_ref, sc_ref, gn_ref, wo_ref, rw_ref, rb_ref,
                    xo_ref, f_ref, wd_ref):
    mix = lax.dot_general(ot_ref[0], wo_ref[:MLA_WIDTH, :], (((0,), (0,)), ((), ())), preferred_element_type=F32)
    mix = mix + jnp.dot(z_ref[0], wo_ref[MLA_WIDTH:, :], preferred_element_type=F32)
    xn = x_ref[0] + g1_ref[0] * mix
    xo_ref[0] = xn
    f = _rms(xn) * gn_ref[...] * (1.0 + sc_ref[0]) + sh_ref[0]
    f_ref[0] = f.astype(f_ref.dtype)
    logits = lax.dot_general(rw_ref[...], f, (((1,), (1,)), ((), ())), preferred_element_type=F32,
                             precision=HIGHEST)
    wd_ref[0] = _router_rows(logits, rb_ref[...])


def _postmix_call(o, z, x, g1, sh, sc, lw, rwt, rb, tm):
    B, N, D = x.shape
    row = lambda b, i: (b, i, 0)
    vec = lambda b, i: (b, 0, 0)
    return pl.pallas_call(
        _postmix_kernel,
        grid=(B, N // tm),
        in_specs=[
            pl.BlockSpec((1, MLA_WIDTH, tm), lambda b, i: (b, 0, i)),
            pl.BlockSpec((1, tm, F_WIDTH), row),
            pl.BlockSpec((1, tm, D), row),
            pl.BlockSpec((1, 1, D), vec),
            pl.BlockSpec((1, 1, D), vec),
            pl.BlockSpec((1, 1, D), vec),
            _full((1, D)),
            _full((D, D)),
            _full((N_EXPERTS, D)),
            _full((N_EXPERTS, 1)),
        ],
        out_specs=[
            pl.BlockSpec((1, tm, D), row),
            pl.BlockSpec((1, tm, D), row),
            pl.BlockSpec((1, ROUTE_ROWS, tm), lambda b, i: (b, 0, i)),
        ],
        out_shape=[
            jax.ShapeDtypeStruct((B, N, D), F32),
            jax.ShapeDtypeStruct((B, N, D), BF16),
            jax.ShapeDtypeStruct((B, ROUTE_ROWS, N), F32),
        ],
        compiler_params=_cparams(("parallel", "parallel")),
        name="postmix_router",
    )(o, z, x, g1, sh, sc, lw["gffn"], lw["wout"], rwt, rb)


def _expert_mlp(xb, wexp, g, wgate_ref, wup_ref, wdown_ref):
    hg = jnp.dot(xb, wgate_ref[g], preferred_element_type=F32)
    hu = jnp.dot(xb, wup_ref[g], preferred_element_type=F32)
    h = (hg * jax.nn.sigmoid(hg)) * hu * wexp
    return jnp.dot(h.astype(BF16), wdown_ref[g], preferred_element_type=F32)


def _moe_kernel(nch_ref, f_ref, rt_ref, x_ref, g2_ref, tri_ref, ex_ref, wgate_ref, wup_ref, wdown_ref, xo_ref, ys_scr):
    tm = f_ref.shape[0]
    tile = pl.program_id(0)
    f = f_ref[...]
    rt = rt_ref[0]
    gid = rt[2 * EPG:2 * EPG + 1, :]
    r16 = lax.broadcasted_iota(jnp.int32, (ROUTE_ROWS, tm), 0)
    hi = rt.astype(BF16).astype(F32)
    whl = jnp.where(r16 < EPG, hi, jnp.where(r16 < 2 * EPG, rt - hi, 0.0)).astype(BF16)
    onehot = (r16.astype(F32) == gid).astype(BF16)
    before = jnp.dot(onehot, tri_ref[...], preferred_element_type=F32)
    pos = jnp.sum(onehot.astype(F32) * before, axis=0, keepdims=True)
    bases = []
    base = 0
    for g in range(N_EGROUPS):
        bases.append(base)
        pos = pos + jnp.where(gid == g, (base * MOE_CHUNK).astype(F32) if g else 0.0, 0.0)
        base = base + nch_ref[tile * N_EGROUPS + g]
    ys_scr[...] = jnp.zeros(ys_scr.shape, ys_scr.dtype)
    slot0 = lax.broadcasted_iota(jnp.int32, (MOE_CHUNK, tm), 0)
    for g in range(N_EGROUPS):

        def chunk(c, carry, g=g):
            row0 = pl.multiple_of((bases[g] + c) * MOE_CHUNK, MOE_CHUNK)
            sel = ((slot0 + row0).astype(F32) == pos).astype(BF16)
            xg = jnp.dot(sel, f, preferred_element_type=F32).astype(BF16)
            wr = lax.dot_general(sel, whl, (((1,), (1,)), ((), ())), preferred_element_type=F32).astype(BF16)
            wexp = jnp.dot(wr, ex_ref[...], preferred_element_type=F32)
            yg = _expert_mlp(xg, wexp, g, wgate_ref, wup_ref, wdown_ref)
            ys_scr[pl.ds(row0, MOE_CHUNK), :] = yg.astype(ys_scr.dtype)
            return carry

        lax.fori_loop(0, nch_ref[tile * N_EGROUPS + g], chunk, 0)
    rows = ys_scr.shape[0]
    scatter = (lax.broadcasted_iota(jnp.int32, (rows, tm), 0).astype(F32) == pos).astype(BF16)
    y = lax.dot_general(scatter, ys_scr[...], (((0,), (0,)), ((), ())), preferred_element_type=F32)
    xo_ref[...] = x_ref[...] + g2_ref[0] * y


MOE_CHUNK = 128


def _moe_call(f, rt, x, g2, lw, tm):
    B, N, D = x.shape
    GW = EPG * D_EXPERT
    tpb = N // tm
    n_tiles = B * tpb
    gid = rt[:, 2 * EPG, :].reshape(n_tiles, tm)
    counts = jnp.stack([jnp.sum(gid == g, axis=1) for g in range(N_EGROUPS)], axis=1)
    nch = ((counts + MOE_CHUNK - 1) // MOE_CHUNK).astype(jnp.int32).reshape(n_tiles * N_EGROUPS)
    tri = jnp.asarray(np.triu(np.ones((tm, tm), np.float32), 1), BF16)
    resident = dict(pipeline_mode=pl.Buffered(1))
    grid_spec = pltpu.PrefetchScalarGridSpec(
        num_scalar_prefetch=1,
        grid=(n_tiles,),
        in_specs=[
            pl.BlockSpec((tm, D), lambda i, o: (i, 0)),
            pl.BlockSpec((1, ROUTE_ROWS, tm), lambda i, o: (i // tpb, 0, i % tpb)),
            pl.BlockSpec((tm, D), lambda i, o: (i, 0)),
            pl.BlockSpec((1, 1, D), lambda i, o: (i // tpb, 0, 0)),
            pl.BlockSpec((tm, tm), lambda i, o: (0, 0), **resident),
            pl.BlockSpec((ROUTE_ROWS, GW), lambda i, o: (0, 0), **resident),
            pl.BlockSpec((N_EGROUPS, D, GW), lambda i, o: (0, 0, 0), **resident),
            pl.BlockSpec((N_EGROUPS, D, GW), lambda i, o: (0, 0, 0), **resident),
            pl.BlockSpec((N_EGROUPS, GW, D), lambda i, o: (0, 0, 0), **resident),
        ],
        out_specs=pl.BlockSpec((tm, D), lambda i, o: (i, 0)),
        scratch_shapes=[pltpu.VMEM((tm + N_EGROUPS * MOE_CHUNK, D), BF16)],
    )
    out = pl.pallas_call(
        _moe_kernel,
        grid_spec=grid_spec,
        out_shape=jax.ShapeDtypeStruct((B * N, D), F32),
        compiler_params=_cparams(("arbitrary",)),
        name="moe_experts",
    )(nch, f.reshape(B * N, D), rt, x.reshape(B * N, D), g2, tri, lw["expand"], lw["wgate"], lw["wup"], lw["wdown"])
    return out.reshape(B, N, D)


def _rope_partner():
    p = np.arange(SLOT)
    for j in range(ROPE_HALF):
        for base in (QK_NOPE, QK_NOPE + ROPE_AXIS):
            p[base + j] = base + ROPE_HALF + j
            p[base + ROPE_HALF + j] = base + j
    return p


def _rope_tables(n):
    f32 = np.float32
    rows = n // GRID_W
    row = np.repeat(np.arange(rows, dtype=f32), GRID_W)
    col = np.tile(np.arange(GRID_W, dtype=f32), rows)
    freqs = np.power(f32(ROPE_BASE), -np.arange(ROPE_HALF, dtype=f32) / f32(ROPE_HALF)).astype(f32)
    ar, ac = (row[:, None] * freqs).astype(f32), (col[:, None] * freqs).astype(f32)
    one = np.ones((n, QK_NOPE), f32)
    zero = np.zeros((n, QK_NOPE), f32)
    pad1 = np.ones((n, SLOT - QK_DIM), f32)
    pad0 = np.zeros((n, SLOT - QK_DIM), f32)
    cos_t = np.concatenate([one, np.cos(ar), np.cos(ar), np.cos(ac), np.cos(ac), pad1], axis=1).astype(f32)
    sin_t = np.concatenate([zero, -np.sin(ar), np.sin(ar), -np.sin(ac), np.sin(ac), pad0], axis=1).astype(f32)
    return jnp.asarray(cos_t), jnp.asarray(sin_t)


def _slot_cols(w, width):
    kdim = w.shape[0]
    w = w.reshape(kdim, N_HEADS, width)
    return jnp.pad(w, ((0, 0), (0, 0), (0, SLOT - width))).reshape(kdim, HSLOTS)


def _layer_weights(l, p, mf_all):
    partner = _rope_partner()
    w_in = p["w_in"][l]
    kr_cols = w_in[:, Q_LORA + KV_LORA:Q_LORA + KV_LORA + QK_ROPE]
    kr_slot = jnp.pad(kr_cols, ((0, 0), (QK_NOPE, SLOT - QK_DIM)))
    kr_swap = jnp.where((np.arange(SLOT) >= QK_NOPE) & (np.arange(SLOT) < QK_DIM), kr_slot[:, partner], 0.0)
    win = jnp.concatenate([w_in[:, :Q_LORA + KV_LORA], kr_slot, kr_swap, w_in[:, Q_LORA + KV_LORA + QK_ROPE:]], axis=1)

    rope_lane = (np.arange(SLOT) >= QK_NOPE) & (np.arange(SLOT) < QK_DIM)
    wq = _slot_cols(p["w_q_b"][l], QK_DIM)
    wq3 = wq.reshape(Q_LORA, N_HEADS, SLOT)
    wqs = jnp.where(rope_lane, wq3[:, :, partner], 0.0).reshape(Q_LORA, HSLOTS)

    wkv = p["w_kv_b"][l].reshape(KV_LORA, N_HEADS, QK_NOPE + V_DIM)
    wk = jnp.pad(wkv[:, :, :QK_NOPE], ((0, 0), (0, 0), (0, SLOT - QK_NOPE))).reshape(KV_LORA, HSLOTS)
    wv = wkv[:, :, QK_NOPE:].reshape(KV_LORA, MLA_WIDTH)

    def gains(g, scale):
        gp = jnp.pad(g, (0, SLOT - QK_DIM))
        gs = jnp.where(rope_lane, gp[partner], 0.0)
        return (jnp.tile(gp, N_HEADS) * scale)[None, :], (jnp.tile(gs, N_HEADS) * scale)[None, :]

    gq, gqs = gains(p["q_norm"][l], QK_DIM ** -0.5 * math.log2(math.e))
    gk, gks = gains(p["k_norm"][l], 1.0)

    def group_cols(w):
        return w.reshape(N_EGROUPS, EPG, D_MODEL, D_EXPERT).transpose(0, 2, 1, 3).reshape(
            N_EGROUPS, D_MODEL, EPG * D_EXPERT).astype(BF16)

    return {
        "gmix": p["norm_mix"][l][None, :],
        "gffn": p["norm_ffn"][l][None, :],
        "win": win.astype(BF16),
        "qa": p["q_a_norm"][l][None, :],
        "wq": wq.astype(BF16),
        "wqs": wqs.astype(BF16),
        "kva": p["kv_a_norm"][l][None, :],
        "wk": wk.astype(BF16),
        "wv": wv.astype(BF16),
        "gq": gq, "gqs": gqs, "gk": gk, "gks": gks,
        "mf": mf_all[l],
        "wout": p["w_out"][l].astype(BF16),
        "wgate": group_cols(p["w_gate"][l]),
        "wup": group_cols(p["w_up"][l]),
        "wdown": p["w_down"][l].reshape(N_EGROUPS, EPG * D_EXPERT, D_MODEL).astype(BF16),
    }


def _const_tables():
    lane_head = np.arange(2 * SLOT) // SLOT
    esum = (lane_head[:, None] == lane_head[None, :]).astype(np.float32)
    ex = np.zeros((ROUTE_ROWS, EPG * D_EXPERT), np.float32)
    for i in range(EPG):
        ex[i, i * D_EXPERT:(i + 1) * D_EXPERT] = 1.0
        ex[EPG + i, i * D_EXPERT:(i + 1) * D_EXPERT] = 1.0
    return jnp.asarray(esum, BF16), jnp.asarray(ex, BF16)


def _mod6(mod_l, rows, B):
    m = mod_l[jnp.asarray(rows)]
    return [m[:, i * D_MODEL:(i + 1) * D_MODEL][:, None, :] for i in range(6)]


def kernel(x, c, ctx, c_ctx, w_ada, b_ada, norm_mix, norm_ffn, w_in, q_a_norm, w_q_b, kv_a_norm, w_kv_b, q_norm,
           k_norm, w_fourier, w_out, router_w, router_bias, w_gate, w_up, w_down):
    B, S, D = x.shape
    n_ctx = ctx.shape[1]
    depth = w_ada.shape[0]
    p = dict(w_in=w_in, q_a_norm=q_a_norm, w_q_b=w_q_b, kv_a_norm=kv_a_norm, w_kv_b=w_kv_b, q_norm=q_norm,
             k_norm=k_norm, norm_mix=norm_mix, norm_ffn=norm_ffn, w_out=w_out, w_gate=w_gate, w_up=w_up,
             w_down=w_down)

    c8 = jnp.concatenate([c, c_ctx[None, :], jnp.zeros((8 - B - 1, D), F32)], axis=0)
    mod = _ada_call(c8, w_ada, b_ada)
    mf_all = _fmat_call(w_fourier)
    esum, expand = _const_tables()
    cos_l, sin_l = _rope_tables(S)
    cos_c = jnp.ones((n_ctx, SLOT), F32)
    sin_c = jnp.zeros((n_ctx, SLOT), F32)
    rwt = router_w.T
    rb = router_bias[:, None]

    tm_lat = 512
    tm_ctx = min(256, n_ctx)
    for l in range(depth):
        last = l == depth - 1
        lw = _layer_weights(l, p, mf_all)
        lw["esum"] = esum
        lw["expand"] = expand
        sh1, sc1, g1, sh2, sc2, g2 = _mod6(mod[l], list(range(B)), B)
        csh1, csc1, cg1, csh2, csc2, cg2 = _mod6(mod[l], [B] * B, B)

        q_c, k_c, v_c, wri_c = _premix_call(ctx, csh1, csc1, lw, cos_c, sin_c, tm_ctx)
        q_l, k_l, v_l, wri_l = _premix_call(x, sh1, sc1, lw, cos_l, sin_l, tm_lat)
        k_all = jnp.concatenate([k_c, k_l], axis=1)
        v_all = jnp.concatenate([v_c, v_l], axis=1)
        bounded = _scores_bounded(q_norm[l], k_norm[l])
        o_l = _attention(q_l, k_all, v_all, ATTN_TQ, bounded)
        z_l = _fft_call(wri_l)
        x_mid, f_l, wd_l = _postmix_call(o_l, z_l, x, g1, sh2, sc2, lw, rwt, rb, tm_lat)
        x = _moe_call(f_l, wd_l, x_mid, g2, lw, tm_lat)
        if not last:
            o_c = _attention(q_c, k_c, v_c, tm_ctx, bounded)
            z_c = _dft_small_call(wri_c)
            c_mid, f_c, wd_c = _postmix_call(o_c, z_c, ctx, cg1, csh2, csc2, lw, rwt, rb, tm_ctx)
            ctx = _moe_call(f_c, wd_c, c_mid, cg2, lw, tm_ctx)
    return x
```

```python
import functools
import math

import numpy as np
import jax
import jax.numpy as jnp
from jax import lax
from jax.experimental import pallas as pl
from jax.experimental.pallas import tpu as pltpu

F32 = jnp.float32
BF16 = jnp.bfloat16
HIGHEST = lax.Precision.HIGHEST

D_MODEL = 1024
N_HEADS = 8
QK_NOPE = 64
QK_ROPE = 32
QK_DIM = QK_NOPE + QK_ROPE
V_DIM = 64
Q_LORA = 256
KV_LORA = 128
ROPE_AXIS = QK_ROPE // 2
ROPE_HALF = ROPE_AXIS // 2
ROPE_BASE = 10000.0
GRID_W = 64
F_GROUPS = 4
F_GDIM = 128
F_WIDTH = F_GROUPS * F_GDIM
MLA_WIDTH = N_HEADS * V_DIM
N_EXPERTS = 16
N_EGROUPS = 4
EPG = N_EXPERTS // N_EGROUPS
D_EXPERT = 256
EPS = 1e-6
ROUTE_ROWS = 16

LANES = 128
SLOT = LANES
HSLOTS = N_HEADS * SLOT
VMEM_LIMIT = 56 * 1024 * 1024

C_CQ = 0
C_CKV = C_CQ + Q_LORA
C_KR = C_CKV + KV_LORA
C_KRS = C_KR + SLOT
C_F = C_KRS + SLOT
IN_COLS = C_F + F_WIDTH


def _cparams(sem):
    return pltpu.CompilerParams(dimension_semantics=sem, vmem_limit_bytes=VMEM_LIMIT)


def _full(shape):
    n = len(shape)
    return pl.BlockSpec(shape, lambda *_: (0,) * n)


def _ada_kernel(c_ref, w_ref, b_ref, o_ref):
    c = c_ref[...]
    s = c * jax.nn.sigmoid(c)
    o_ref[0] = jnp.dot(s, w_ref[0], preferred_element_type=F32, precision=HIGHEST) + b_ref[0]


def _ada_call(c8, w_ada, b_ada):
    L, D, W = w_ada.shape
    tn = 1536
    return pl.pallas_call(
        _ada_kernel,
        grid=(L, W // tn),
        in_specs=[
            pl.BlockSpec((8, D), lambda l, j: (0, 0)),
            pl.BlockSpec((1, D, tn), lambda l, j: (l, 0, j)),
            pl.BlockSpec((1, 1, tn), lambda l, j: (l, 0, j)),
        ],
        out_specs=pl.BlockSpec((1, 8, tn), lambda l, j: (l, 0, j)),
        out_shape=jax.ShapeDtypeStruct((L, 8, W), F32),
        compiler_params=_cparams(("parallel", "parallel")),
        name="adaln_mod",
    )(c8, w_ada, b_ada.reshape(L, 1, W))


def _fmat_kernel(cc_ref, sc_ref, wf_ref, o_ref):
    o_ref[...] = jnp.zeros(o_ref.shape, o_ref.dtype)
    for g in range(F_GROUPS):
        wf = wf_ref[0, g]
        mr = jnp.dot(cc_ref[...], wf, preferred_element_type=F32, precision=HIGHEST)
        mi = jnp.dot(sc_ref[...], wf, preferred_element_type=F32, precision=HIGHEST)
        r0 = g * F_GDIM
        o_ref[0, r0:r0 + F_GDIM, r0:r0 + F_GDIM] = mr.astype(o_ref.dtype)
        o_ref[0, r0:r0 + F_GDIM, F_WIDTH + r0:F_WIDTH + r0 + F_GDIM] = mi.astype(o_ref.dtype)


def _fmat_call(w_fourier):
    L = w_fourier.shape[0]
    c = np.arange(F_GDIM)
    ang = 2.0 * np.pi * np.outer(c, c) / F_GDIM
    cc = jnp.asarray((np.cos(ang) / math.sqrt(F_GDIM)).astype(np.float32))
    sc = jnp.asarray((-np.sin(ang) / math.sqrt(F_GDIM)).astype(np.float32))
    return pl.pallas_call(
        _fmat_kernel,
        grid=(L,),
        in_specs=[
            _full((F_GDIM, F_GDIM)),
            _full((F_GDIM, F_GDIM)),
            pl.BlockSpec((1, F_GROUPS, F_GDIM, F_GDIM), lambda l: (l, 0, 0, 0)),
        ],
        out_specs=pl.BlockSpec((1, F_WIDTH, 2 * F_WIDTH), lambda l: (l, 0, 0)),
        out_shape=jax.ShapeDtypeStruct((L, F_WIDTH, 2 * F_WIDTH), BF16),
        compiler_params=_cparams(("parallel",)),
        name="fourier_chan_mats",
    )(cc, sc, w_fourier)


def _rms(x):
    return x * lax.rsqrt(jnp.mean(x * x, axis=-1, keepdims=True) + EPS)


def _tile8(a):
    return jnp.concatenate([a] * N_HEADS, axis=-1)


def _head_sums(x2, e2):
    w = e2.shape[0]
    xb = x2.astype(BF16)
    return jnp.concatenate(
        [jnp.dot(xb[:, c * w:(c + 1) * w], e2, preferred_element_type=F32) for c in range(HSLOTS // w)], axis=-1)


def _premix_kernel(x_ref, sh_ref, sc_ref, gn_ref, win_ref, qa_ref, wq_ref, wqs_ref, kva_ref, wk_ref, wv_ref,
                   e_ref, cos_ref, sin_ref, gq_ref, gqs_ref, gk_ref, gks_ref, mf_ref,
                   q_ref, k_ref, v_ref, wri_ref):
    x = x_ref[0]
    h = _rms(x) * gn_ref[...] * (1.0 + sc_ref[0]) + sh_ref[0]
    proj = jnp.dot(h.astype(BF16), win_ref[...], preferred_element_type=F32)
    cq = proj[:, C_CQ:C_CKV]
    ckv = proj[:, C_CKV:C_KR]
    kr = proj[:, C_KR:C_KRS]
    krs = proj[:, C_KRS:C_F]
    f = proj[:, C_F:IN_COLS]

    cos8 = _tile8(cos_ref[...])
    sin8 = _tile8(sin_ref[...])
    inv_d = 1.0 / QK_DIM

    cqn = (_rms(cq) * qa_ref[...]).astype(BF16)
    qr = jnp.dot(cqn, wq_ref[...], preferred_element_type=F32)
    qs = jnp.dot(cqn, wqs_ref[...], preferred_element_type=F32)
    ssq = _head_sums(qr * qr, e_ref[...])
    rq = lax.rsqrt(ssq * inv_d + EPS)
    q = rq * (qr * (gq_ref[...] * cos8) + qs * (gqs_ref[...] * sin8))
    q_ref[0] = q.astype(q_ref.dtype)

    ckvn = (_rms(ckv) * kva_ref[...]).astype(BF16)
    kraw = jnp.dot(ckvn, wk_ref[...], preferred_element_type=F32) + _tile8(kr)
    ssk = _head_sums(kraw * kraw, e_ref[...])
    rk = lax.rsqrt(ssk * inv_d + EPS)
    k = rk * (kraw * (gk_ref[...] * cos8) + _tile8(krs) * (gks_ref[...] * sin8))
    k_ref[0] = k.astype(k_ref.dtype)
    v_ref[0] = jnp.dot(ckvn, wv_ref[...], preferred_element_type=F32).astype(v_ref.dtype)

    w = jnp.dot(f.astype(BF16), mf_ref[...], preferred_element_type=F32)
    wri_ref[0, 0] = w[:, :F_WIDTH].astype(wri_ref.dtype)
    wri_ref[0, 1] = w[:, F_WIDTH:].astype(wri_ref.dtype)


def _premix_call(x, sh, sc, lw, cos_t, sin_t, tm):
    B, N, D = x.shape
    row = lambda b, i: (b, i, 0)
    vec = lambda b, i: (b, 0, 0)
    in_specs = [
        pl.BlockSpec((1, tm, D), row),
        pl.BlockSpec((1, 1, D), vec),
        pl.BlockSpec((1, 1, D), vec),
        _full((1, D)),
        _full((D, IN_COLS)),
        _full((1, Q_LORA)),
        _full((Q_LORA, HSLOTS)),
        _full((Q_LORA, HSLOTS)),
        _full((1, KV_LORA)),
        _full((KV_LORA, HSLOTS)),
        _full((KV_LORA, MLA_WIDTH)),
        _full((2 * SLOT, 2 * SLOT)),
        pl.BlockSpec((tm, SLOT), lambda b, i: (i, 0)),
        pl.BlockSpec((tm, SLOT), lambda b, i: (i, 0)),
        _full((1, HSLOTS)),
        _full((1, HSLOTS)),
        _full((1, HSLOTS)),
        _full((1, HSLOTS)),
        _full((F_WIDTH, 2 * F_WIDTH)),
    ]
    out_specs = [
        pl.BlockSpec((1, tm, HSLOTS), row),
        pl.BlockSpec((1, tm, HSLOTS), row),
        pl.BlockSpec((1, tm, MLA_WIDTH), row),
        pl.BlockSpec((1, 2, tm, F_WIDTH), lambda b, i: (b, 0, i, 0)),
    ]
    out_shape = [
        jax.ShapeDtypeStruct((B, N, HSLOTS), BF16),
        jax.ShapeDtypeStruct((B, N, HSLOTS), BF16),
        jax.ShapeDtypeStruct((B, N, MLA_WIDTH), BF16),
        jax.ShapeDtypeStruct((B, 2, N, F_WIDTH), BF16),
    ]
    return pl.pallas_call(
        _premix_kernel,
        grid=(B, N // tm),
        in_specs=in_specs,
        out_specs=out_specs,
        out_shape=out_shape,
        compiler_params=_cparams(("parallel", "parallel")),
        name="premix_proj",
    )(x, sh, sc, lw["gmix"], lw["win"], lw["qa"], lw["wq"], lw["wqs"], lw["kva"], lw["wk"], lw["wv"],
      lw["esum"], cos_t, sin_t, lw["gq"], lw["gqs"], lw["gk"], lw["gks"], lw["mf"])


def _attn_kernel(q_ref, k_ref, v_ref, ot_ref, vt_scr, qt_scr, *, n_kt, bounded):
    tq = q_ref.shape[1]
    tk = ATTN_TK

    @pl.when(pl.program_id(2) == 0)
    def _():
        odd = (pl.program_id(1) % 2) == 1
        for t in range(n_kt):
            vt_pair = v_ref[0, t * tk:(t + 1) * tk, :].T
            vt_scr[t] = jnp.where(odd, vt_pair[V_DIM:], vt_pair[:V_DIM])

    qt_scr[...] = q_ref[0].T
    qt = qt_scr[...]

    def scores(t):
        off = t * tk if isinstance(t, int) else pl.multiple_of(t * tk, tk)
        return jnp.dot(k_ref[0, pl.ds(off, tk), :], qt, preferred_element_type=F32)

    def pv(t, p):
        return jnp.dot(vt_scr[t], p.astype(BF16), preferred_element_type=F32)

    if bounded:
        acc = jnp.zeros((V_DIM, tq), F32)
        den = jnp.zeros((1, tq), F32)
        ahead = [scores(t) for t in range(min(ATTN_LOOKAHEAD, n_kt))]
        for t in range(n_kt):
            if t + ATTN_LOOKAHEAD < n_kt:
                ahead.append(scores(t + ATTN_LOOKAHEAD))
            p = jnp.exp2(ahead.pop(0))
            den = den + jnp.sum(p, axis=0, keepdims=True)
            acc = acc + pv(t, p)
    else:
        unroll = _attn_unroll(n_kt)

        def body(g, carry):
            m, den, acc, s = carry
            for u in range(unroll):
                t = g * unroll + u
                s_next = scores(jnp.minimum(t + 1, n_kt - 1))
                m_new = jnp.maximum(m, jnp.max(s, axis=0, keepdims=True))
                alpha = jnp.exp2(m - m_new)
                p = jnp.exp2(s - m_new)
                den = alpha * den + jnp.sum(p, axis=0, keepdims=True)
                acc = alpha * acc + pv(t, p)
                m, s = m_new, s_next
            return m, den, acc, s

        init = (jnp.full((1, tq), -1e30, F32), jnp.zeros((1, tq), F32), jnp.zeros((V_DIM, tq), F32), scores(0))
        _, den, acc, _ = lax.fori_loop(0, n_kt // unroll, body, init)
    ot_ref[0] = (acc / den).astype(ot_ref.dtype)


ATTN_TK = 256
ATTN_TQ = 512
ATTN_LOOKAHEAD = 2
ATTN_SCORE_BOUND = 60.0


def _attn_unroll(n_kt):
    for u in (13, 9, 8, 5, 4, 3, 2):
        if n_kt % u == 0:
            return u
    return 1


def _attn_call(q, k, v, tq, bounded):
    B, Nq, _ = q.shape
    Nk = k.shape[1]
    n_kt = Nk // ATTN_TK
    kern = functools.partial(_attn_kernel, n_kt=n_kt, bounded=bounded)
    return pl.pallas_call(
        kern,
        grid=(B, N_HEADS, Nq // tq),
        in_specs=[
            pl.BlockSpec((1, tq, SLOT), lambda b, h, i: (b, i, h)),
            pl.BlockSpec((1, Nk, SLOT), lambda b, h, i: (b, 0, h)),
            pl.BlockSpec((1, Nk, 2 * V_DIM), lambda b, h, i: (b, 0, h // 2)),
        ],
        out_specs=pl.BlockSpec((1, V_DIM, tq), lambda b, h, i: (b, h, i)),
        out_shape=jax.ShapeDtypeStruct((B, MLA_WIDTH, Nq), BF16),
        scratch_shapes=[pltpu.VMEM((n_kt, V_DIM, ATTN_TK), BF16), pltpu.VMEM((SLOT, tq), BF16)],
        compiler_params=_cparams(("arbitrary", "arbitrary", "arbitrary")),
        name="mla_attention_bounded" if bounded else "mla_attention_online",
    )(q, k, v)


def _attention(q, k, v, tq, scores_bounded):
    return lax.cond(scores_bounded,
                    lambda a, b, c: _attn_call(a, b, c, tq, True),
                    lambda a, b, c: _attn_call(a, b, c, tq, False),
                    q, k, v)


def _scores_bounded(q_gain, k_gain):
    bound = math.sqrt(QK_DIM) * math.log2(math.e) * jnp.max(jnp.abs(q_gain)) * jnp.max(jnp.abs(k_gain))
    return bound <= ATTN_SCORE_BOUND


def _fft1_kernel(f1_ref, tr_ref, ti_ref, w_ref, y_ref, *, n1, n2_per):
    y = jnp.dot(f1_ref[...], w_ref[0], preferred_element_type=F32)
    yr = y[:n1]
    yi = y[n1:]
    for t in range(n2_per):
        tr = tr_ref[:, t * LANES:(t + 1) * LANES]
        ti = ti_ref[:, t * LANES:(t + 1) * LANES]
        tr4 = jnp.concatenate([tr] * F_GROUPS, axis=-1)
        ti4 = jnp.concatenate([ti] * F_GROUPS, axis=-1)
        a = yr[:, t * F_WIDTH:(t + 1) * F_WIDTH]
        b = yi[:, t * F_WIDTH:(t + 1) * F_WIDTH]
        y_ref[0, :n1, t * F_WIDTH:(t + 1) * F_WIDTH] = (a * tr4 - b * ti4).astype(y_ref.dtype)
        y_ref[0, n1:, t * F_WIDTH:(t + 1) * F_WIDTH] = (a * ti4 + b * tr4).astype(y_ref.dtype)


def _fft2_kernel(c2_ref, s2_ref, y_ref, o_ref, *, kc):
    for i in range(kc):
        re = jnp.dot(c2_ref[...], y_ref[0, 0, i], preferred_element_type=F32)
        im = jnp.dot(s2_ref[...], y_ref[0, 1, i], preferred_element_type=F32)
        o_ref[0, :, i * F_WIDTH:(i + 1) * F_WIDTH] = (re + im).astype(o_ref.dtype)


def _fft_consts(n):
    n2 = LANES
    n1 = n // n2
    a1 = 2.0 * np.pi * np.outer(np.arange(n1), np.arange(n1)) / n1
    c1, s1 = np.cos(a1), np.sin(a1)
    f1 = np.block([[c1, s1], [-s1, c1]]).astype(np.float32)
    at = 2.0 * np.pi * np.outer(np.arange(n1), np.arange(n2)) / n
    tr = np.repeat(np.cos(at), LANES, axis=1).astype(np.float32)
    ti = np.repeat(-np.sin(at), LANES, axis=1).astype(np.float32)
    a2 = 2.0 * np.pi * np.outer(np.arange(n2), np.arange(n2)) / n2
    scale = 1.0 / math.sqrt(n)
    c2 = (np.cos(a2) * scale).astype(np.float32)
    s2 = (np.sin(a2) * scale).astype(np.float32)
    return n1, n2, f1, tr, ti, c2, s2


def _fft_call(wri):
    B, _, N, _ = wri.shape
    n1, n2, f1, tr, ti, c2, s2 = _fft_consts(N)
    cols = n2 * F_WIDTH
    n2_per = 16
    tc = n2_per * F_WIDTH
    y = pl.pallas_call(
        functools.partial(_fft1_kernel, n1=n1, n2_per=n2_per),
        grid=(B, cols // tc),
        in_specs=[
            _full((2 * n1, 2 * n1)),
            pl.BlockSpec((n1, n2_per * LANES), lambda b, j: (0, j)),
            pl.BlockSpec((n1, n2_per * LANES), lambda b, j: (0, j)),
            pl.BlockSpec((1, 2 * n1, tc), lambda b, j: (b, 0, j)),
        ],
        out_specs=pl.BlockSpec((1, 2 * n1, tc), lambda b, j: (b, 0, j)),
        out_shape=jax.ShapeDtypeStruct((B, 2 * n1, cols), BF16),
        compiler_params=_cparams(("parallel", "parallel")),
        name="fft_stage1",
    )(jnp.asarray(f1, BF16), jnp.asarray(tr), jnp.asarray(ti), wri.reshape(B, 2 * n1, cols))
    kc = min(16, n1)
    out = pl.pallas_call(
        functools.partial(_fft2_kernel, kc=kc),
        grid=(B, n1 // kc),
        in_specs=[
            _full((n2, n2)),
            _full((n2, n2)),
            pl.BlockSpec((1, 2, kc, n2, F_WIDTH), lambda b, j: (b, 0, j, 0, 0)),
        ],
        out_specs=pl.BlockSpec((1, n2, kc * F_WIDTH), lambda b, j: (b, 0, j)),
        out_shape=jax.ShapeDtypeStruct((B, n2, n1 * F_WIDTH), BF16),
        compiler_params=_cparams(("parallel", "parallel")),
        name="fft_stage2",
    )(jnp.asarray(c2, BF16), jnp.asarray(s2, BF16), y.reshape(B, 2, n1, n2, F_WIDTH))
    return out.reshape(B, N, F_WIDTH)


def _dft_small_kernel(c_ref, s_ref, w_ref, o_ref):
    re = jnp.dot(c_ref[...], w_ref[0, 0], preferred_element_type=F32)
    im = jnp.dot(s_ref[...], w_ref[0, 1], preferred_element_type=F32)
    o_ref[0] = (re + im).astype(o_ref.dtype)


def _dft_small_call(wri):
    B, _, N, _ = wri.shape
    a = 2.0 * np.pi * np.outer(np.arange(N), np.arange(N)) / N
    c = (np.cos(a) / math.sqrt(N)).astype(np.float32)
    s = (np.sin(a) / math.sqrt(N)).astype(np.float32)
    return pl.pallas_call(
        _dft_small_kernel,
        grid=(B,),
        in_specs=[_full((N, N)), _full((N, N)), pl.BlockSpec((1, 2, N, F_WIDTH), lambda b: (b, 0, 0, 0))],
        out_specs=pl.BlockSpec((1, N, F_WIDTH), lambda b: (b, 0, 0)),
        out_shape=jax.ShapeDtypeStruct((B, N, F_WIDTH), BF16),
        compiler_params=_cparams(("parallel",)),
        name="dft_small",
    )(jnp.asarray(c, BF16), jnp.asarray(s, BF16), wri)


def _router_rows(logits, bias):
    t = logits.shape[1]
    s = jax.nn.sigmoid(logits)
    b = s + bias
    srow = [s[e:e + 1, :] for e in range(N_EXPERTS)]
    brow = [b[e:e + 1, :] for e in range(N_EXPERTS)]
    gscore = []
    for g in range(N_EGROUPS):
        b0, b1, b2, b3 = brow[EPG * g:EPG * g + EPG]
        m1, n1 = jnp.maximum(b0, b1), jnp.minimum(b0, b1)
        m2, n2 = jnp.maximum(b2, b3), jnp.minimum(b2, b3)
        top1 = jnp.maximum(m1, m2)
        top2 = jnp.maximum(jnp.minimum(m1, m2), jnp.maximum(n1, n2))
        gscore.append(top1 + top2)
    best = gscore[0]
    gi = jnp.zeros((1, t), jnp.int32)
    for g in range(1, N_EGROUPS):
        take = gscore[g] > best
        gi = jnp.where(take, g, gi)
        best = jnp.where(take, gscore[g], best)
    ib, isc = [], []
    for i in range(EPG):
        vb, vs = brow[i], srow[i]
        for g in range(1, N_EGROUPS):
            sel = gi == g
            vb = jnp.where(sel, brow[EPG * g + i], vb)
            vs = jnp.where(sel, srow[EPG * g + i], vs)
        ib.append(vb)
        isc.append(vs)
    w = []
    for i in range(EPG):
        rank = jnp.zeros((1, t), jnp.int32)
        for j in range(EPG):
            if j == i:
                continue
            ahead = (ib[j] > ib[i]) | ((ib[j] == ib[i]) & (j < i))
            rank = rank + ahead.astype(jnp.int32)
        w.append(jnp.where(rank < 2, isc[i], 0.0))
    tot = (w[0] + w[1]) + (w[2] + w[3])
    inv = 1.0 / tot
    ridx = lax.broadcasted_iota(jnp.int32, (ROUTE_ROWS, t), 0)
    out = jnp.where(ridx == 2 * EPG, gi.astype(F32), 0.0)
    for i in range(EPG):
        out = jnp.where((ridx == i) | (ridx == EPG + i), w[i] * inv, out)
    return out


def _postmix_kernel(ot_ref, z_ref, x_ref, g1_ref, sh_ref, sc_ref, gn_ref, wo_ref, rw_ref, rb_ref,
                    xo_ref, f_ref, wd_ref):
    mix = lax.dot_general(ot_ref[0], wo_ref[:MLA_WIDTH, :], (((0,), (0,)), ((), ())), preferred_element_type=F32)
    mix = mix + jnp.dot(z_ref[0], wo_ref[MLA_WIDTH:, :], preferred_element_type=F32)
    xn = x_ref[0] + g1_ref[0] * mix
    xo_ref[0] = xn
    f = _rms(xn) * gn_ref[...] * (1.0 + sc_ref[0]) + sh_ref[0]
    f_ref[0] = f.astype(f_ref.dtype)
    logits = lax.dot_general(rw_ref[...], f, (((1,), (1,)), ((), ())), preferred_element_type=F32,
                             precision=HIGHEST)
    wd_ref[0] = _router_rows(logits, rb_ref[...])


def _postmix_call(o, z, x, g1, sh, sc, lw, rwt, rb, tm):
    B, N, D = x.shape
    row = lambda b, i: (b, i, 0)
    vec = lambda b, i: (b, 0, 0)
    return pl.pallas_call(
        _postmix_kernel,
        grid=(B, N // tm),
        in_specs=[
            pl.BlockSpec((1, MLA_WIDTH, tm), lambda b, i: (b, 0, i)),
            pl.BlockSpec((1, tm, F_WIDTH), row),
            pl.BlockSpec((1, tm, D), row),
            pl.BlockSpec((1, 1, D), vec),
            pl.BlockSpec((1, 1, D), vec),
            pl.BlockSpec((1, 1, D), vec),
            _full((1, D)),
            _full((D, D)),
            _full((N_EXPERTS, D)),
            _full((N_EXPERTS, 1)),
        ],
        out_specs=[
            pl.BlockSpec((1, tm, D), row),
            pl.BlockSpec((1, tm, D), row),
            pl.BlockSpec((1, ROUTE_ROWS, tm), lambda b, i: (b, 0, i)),
        ],
        out_shape=[
            jax.ShapeDtypeStruct((B, N, D), F32),
            jax.ShapeDtypeStruct((B, N, D), BF16),
            jax.ShapeDtypeStruct((B, ROUTE_ROWS, N), F32),
        ],
        compiler_params=_cparams(("parallel", "parallel")),
        name="postmix_router",
    )(o, z, x, g1, sh, sc, lw["gffn"], lw["wout"], rwt, rb)


def _expert_mlp(xb, wexp, g, wgate_ref, wup_ref, wdown_ref):
    hg = jnp.dot(xb, wgate_ref[g], preferred_element_type=F32)
    hu = jnp.dot(xb, wup_ref[g], preferred_element_type=F32)
    h = (hg * jax.nn.sigmoid(hg)) * hu * wexp
    return jnp.dot(h.astype(BF16), wdown_ref[g], preferred_element_type=F32)


def _moe_kernel(nch_ref, f_ref, rt_ref, x_ref, g2_ref, tri_ref, ex_ref, wgate_ref, wup_ref, wdown_ref, xo_ref, ys_scr):
    tm = f_ref.shape[0]
    tile = pl.program_id(0)
    f = f_ref[...]
    rt = rt_ref[0]
    gid = rt[2 * EPG:2 * EPG + 1, :]
    r16 = lax.broadcasted_iota(jnp.int32, (ROUTE_ROWS, tm), 0)
    hi = rt.astype(BF16).astype(F32)
    whl = jnp.where(r16 < EPG, hi, jnp.where(r16 < 2 * EPG, rt - hi, 0.0)).astype(BF16)
    onehot = (r16.astype(F32) == gid).astype(BF16)
    before = jnp.dot(onehot, tri_ref[...], preferred_element_type=F32)
    pos = jnp.sum(onehot.astype(F32) * before, axis=0, keepdims=True)
    bases = []
    base = 0
    for g in range(N_EGROUPS):
        bases.append(base)
        pos = pos + jnp.where(gid == g, (base * MOE_CHUNK).astype(F32) if g else 0.0, 0.0)
        base = base + nch_ref[tile * N_EGROUPS + g]
    ys_scr[...] = jnp.zeros(ys_scr.shape, ys_scr.dtype)
    slot0 = lax.broadcasted_iota(jnp.int32, (MOE_CHUNK, tm), 0)
    for g in range(N_EGROUPS):

        def chunk(c, carry, g=g):
            row0 = pl.multiple_of((bases[g] + c) * MOE_CHUNK, MOE_CHUNK)
            sel = ((slot0 + row0).astype(F32) == pos).astype(BF16)
            xg = jnp.dot(sel, f, preferred_element_type=F32).astype(BF16)
            wr = lax.dot_general(sel, whl, (((1,), (1,)), ((), ())), preferred_element_type=F32).astype(BF16)
            wexp = jnp.dot(wr, ex_ref[...], preferred_element_type=F32)
            yg = _expert_mlp(xg, wexp, g, wgate_ref, wup_ref, wdown_ref)
            ys_scr[pl.ds(row0, MOE_CHUNK), :] = yg.astype(ys_scr.dtype)
            return carry

        lax.fori_loop(0, nch_ref[tile * N_EGROUPS + g], chunk, 0)
    rows = ys_scr.shape[0]
    scatter = (lax.broadcasted_iota(jnp.int32, (rows, tm), 0).astype(F32) == pos).astype(BF16)
    y = lax.dot_general(scatter, ys_scr[...], (((0,), (0,)), ((), ())), preferred_element_type=F32)
    xo_ref[...] = x_ref[...] + g2_ref[0] * y


MOE_CHUNK = 128


def _moe_call(f, rt, x, g2, lw, tm):
    B, N, D = x.shape
    GW = EPG * D_EXPERT
    tpb = N // tm
    n_tiles = B * tpb
    gid = rt[:, 2 * EPG, :].reshape(n_tiles, tm)
    counts = jnp.stack([jnp.sum(gid == g, axis=1) for g in range(N_EGROUPS)], axis=1)
    nch = ((counts + MOE_CHUNK - 1) // MOE_CHUNK).astype(jnp.int32).reshape(n_tiles * N_EGROUPS)
    tri = jnp.asarray(np.triu(np.ones((tm, tm), np.float32), 1), BF16)
    resident = dict(pipeline_mode=pl.Buffered(1))
    grid_spec = pltpu.PrefetchScalarGridSpec(
        num_scalar_prefetch=1,
        grid=(n_tiles,),
        in_specs=[
            pl.BlockSpec((tm, D), lambda i, o: (i, 0)),
            pl.BlockSpec((1, ROUTE_ROWS, tm), lambda i, o: (i // tpb, 0, i % tpb)),
            pl.BlockSpec((tm, D), lambda i, o: (i, 0)),
            pl.BlockSpec((1, 1, D), lambda i, o: (i // tpb, 0, 0)),
            pl.BlockSpec((tm, tm), lambda i, o: (0, 0), **resident),
            pl.BlockSpec((ROUTE_ROWS, GW), lambda i, o: (0, 0), **resident),
            pl.BlockSpec((N_EGROUPS, D, GW), lambda i, o: (0, 0, 0), **resident),
            pl.BlockSpec((N_EGROUPS, D, GW), lambda i, o: (0, 0, 0), **resident),
            pl.BlockSpec((N_EGROUPS, GW, D), lambda i, o: (0, 0, 0), **resident),
        ],
        out_specs=pl.BlockSpec((tm, D), lambda i, o: (i, 0)),
        scratch_shapes=[pltpu.VMEM((tm + N_EGROUPS * MOE_CHUNK, D), BF16)],
    )
    out = pl.pallas_call(
        _moe_kernel,
        grid_spec=grid_spec,
        out_shape=jax.ShapeDtypeStruct((B * N, D), F32),
        compiler_params=_cparams(("arbitrary",)),
        name="moe_experts",
    )(nch, f.reshape(B * N, D), rt, x.reshape(B * N, D), g2, tri, lw["expand"], lw["wgate"], lw["wup"], lw["wdown"])
    return out.reshape(B, N, D)


def _rope_partner():
    p = np.arange(SLOT)
    for j in range(ROPE_HALF):
        for base in (QK_NOPE, QK_NOPE + ROPE_AXIS):
            p[base + j] = base + ROPE_HALF + j
            p[base + ROPE_HALF + j] = base + j
    return p


def _rope_tables(n):
    f32 = np.float32
    rows = n // GRID_W
    row = np.repeat(np.arange(rows, dtype=f32), GRID_W)
    col = np.tile(np.arange(GRID_W, dtype=f32), rows)
    freqs = np.power(f32(ROPE_BASE), -np.arange(ROPE_HALF, dtype=f32) / f32(ROPE_HALF)).astype(f32)
    ar, ac = (row[:, None] * freqs).astype(f32), (col[:, None] * freqs).astype(f32)
    one = np.ones((n, QK_NOPE), f32)
    zero = np.zeros((n, QK_NOPE), f32)
    pad1 = np.ones((n, SLOT - QK_DIM), f32)
    pad0 = np.zeros((n, SLOT - QK_DIM), f32)
    cos_t = np.concatenate([one, np.cos(ar), np.cos(ar), np.cos(ac), np.cos(ac), pad1], axis=1).astype(f32)
    sin_t = np.concatenate([zero, -np.sin(ar), np.sin(ar), -np.sin(ac), np.sin(ac), pad0], axis=1).astype(f32)
    return jnp.asarray(cos_t), jnp.asarray(sin_t)


def _slot_cols(w, width):
    kdim = w.shape[0]
    w = w.reshape(kdim, N_HEADS, width)
    return jnp.pad(w, ((0, 0), (0, 0), (0, SLOT - width))).reshape(kdim, HSLOTS)


def _layer_weights(l, p, mf_all):
    partner = _rope_partner()
    w_in = p["w_in"][l]
    kr_cols = w_in[:, Q_LORA + KV_LORA:Q_LORA + KV_LORA + QK_ROPE]
    kr_slot = jnp.pad(kr_cols, ((0, 0), (QK_NOPE, SLOT - QK_DIM)))
    kr_swap = jnp.where((np.arange(SLOT) >= QK_NOPE) & (np.arange(SLOT) < QK_DIM), kr_slot[:, partner], 0.0)
    win = jnp.concatenate([w_in[:, :Q_LORA + KV_LORA], kr_slot, kr_swap, w_in[:, Q_LORA + KV_LORA + QK_ROPE:]], axis=1)

    rope_lane = (np.arange(SLOT) >= QK_NOPE) & (np.arange(SLOT) < QK_DIM)
    wq = _slot_cols(p["w_q_b"][l], QK_DIM)
    wq3 = wq.reshape(Q_LORA, N_HEADS, SLOT)
    wqs = jnp.where(rope_lane, wq3[:, :, partner], 0.0).reshape(Q_LORA, HSLOTS)

    wkv = p["w_kv_b"][l].reshape(KV_LORA, N_HEADS, QK_NOPE + V_DIM)
    wk = jnp.pad(wkv[:, :, :QK_NOPE], ((0, 0), (0, 0), (0, SLOT - QK_NOPE))).reshape(KV_LORA, HSLOTS)
    wv = wkv[:, :, QK_NOPE:].reshape(KV_LORA, MLA_WIDTH)

    def gains(g, scale):
        gp = jnp.pad(g, (0, SLOT - QK_DIM))
        gs = jnp.where(rope_lane, gp[partner], 0.0)
        return (jnp.tile(gp, N_HEADS) * scale)[None, :], (jnp.tile(gs, N_HEADS) * scale)[None, :]

    gq, gqs = gains(p["q_norm"][l], QK_DIM ** -0.5 * math.log2(math.e))
    gk, gks = gains(p["k_norm"][l], 1.0)

    def group_cols(w):
        return w.reshape(N_EGROUPS, EPG, D_MODEL, D_EXPERT).transpose(0, 2, 1, 3).reshape(
            N_EGROUPS, D_MODEL, EPG * D_EXPERT).astype(BF16)

    return {
        "gmix": p["norm_mix"][l][None, :],
        "gffn": p["norm_ffn"][l][None, :],
        "win": win.astype(BF16),
        "qa": p["q_a_norm"][l][None, :],
        "wq": wq.astype(BF16),
        "wqs": wqs.astype(BF16),
        "kva": p["kv_a_norm"][l][None, :],
        "wk": wk.astype(BF16),
        "wv": wv.astype(BF16),
        "gq": gq, "gqs": gqs, "gk": gk, "gks": gks,
        "mf": mf_all[l],
        "wout": p["w_out"][l].astype(BF16),
        "wgate": group_cols(p["w_gate"][l]),
        "wup": group_cols(p["w_up"][l]),
        "wdown": p["w_down"][l].reshape(N_EGROUPS, EPG * D_EXPERT, D_MODEL).astype(BF16),
    }


def _const_tables():
    lane_head = np.arange(2 * SLOT) // SLOT
    esum = (lane_head[:, None] == lane_head[None, :]).astype(np.float32)
    ex = np.zeros((ROUTE_ROWS, EPG * D_EXPERT), np.float32)
    for i in range(EPG):
        ex[i, i * D_EXPERT:(i + 1) * D_EXPERT] = 1.0
        ex[EPG + i, i * D_EXPERT:(i + 1) * D_EXPERT] = 1.0
    return jnp.asarray(esum, BF16), jnp.asarray(ex, BF16)


def _mod6(mod_l, rows, B):
    m = mod_l[jnp.asarray(rows)]
    return [m[:, i * D_MODEL:(i + 1) * D_MODEL][:, None, :] for i in range(6)]


def kernel(x, c, ctx, c_ctx, w_ada, b_ada, norm_mix, norm_ffn, w_in, q_a_norm, w_q_b, kv_a_norm, w_kv_b, q_norm,
           k_norm, w_fourier, w_out, router_w, router_bias, w_gate, w_up, w_down):
    B, S, D = x.shape
    n_ctx = ctx.shape[1]
    depth = w_ada.shape[0]
    p = dict(w_in=w_in, q_a_norm=q_a_norm, w_q_b=w_q_b, kv_a_norm=kv_a_norm, w_kv_b=w_kv_b, q_norm=q_norm,
             k_norm=k_norm, norm_mix=norm_mix, norm_ffn=norm_ffn, w_out=w_out, w_gate=w_gate, w_up=w_up,
             w_down=w_down)

    c8 = jnp.concatenate([c, c_ctx[None, :], jnp.zeros((8 - B - 1, D), F32)], axis=0)
    mod = _ada_call(c8, w_ada, b_ada)
    mf_all = _fmat_call(w_fourier)
    esum, expand = _const_tables()
    cos_l, sin_l = _rope_tables(S)
    cos_c = jnp.ones((n_ctx, SLOT), F32)
    sin_c = jnp.zeros((n_ctx, SLOT), F32)
    rwt = router_w.T
    rb = router_bias[:, None]

    tm_lat = 512
    tm_ctx = min(256, n_ctx)
    for l in range(depth):
        last = l == depth - 1
        lw = _layer_weights(l, p, mf_all)
        lw["esum"] = esum
        lw["expand"] = expand
        sh1, sc1, g1, sh2, sc2, g2 = _mod6(mod[l], list(range(B)), B)
        csh1, csc1, cg1, csh2, csc2, cg2 = _mod6(mod[l], [B] * B, B)

        q_c, k_c, v_c, wri_c = _premix_call(ctx, csh1, csc1, lw, cos_c, sin_c, tm_ctx)
        q_l, k_l, v_l, wri_l = _premix_call(x, sh1, sc1, lw, cos_l, sin_l, tm_lat)
        k_all = jnp.concatenate([k_c, k_l], axis=1)
        v_all = jnp.concatenate([v_c, v_l], axis=1)
        bounded = _scores_bounded(q_norm[l], k_norm[l])
        o_l = _attention(q_l, k_all, v_all, ATTN_TQ, bounded)
        z_l = _fft_call(wri_l)
        x_mid, f_l, wd_l = _postmix_call(o_l, z_l, x, g1, sh2, sc2, lw, rwt, rb, tm_lat)
        x = _moe_call(f_l, wd_l, x_mid, g2, lw, tm_lat)
        if not last:
            o_c = _attention(q_c, k_c, v_c, tm_ctx, bounded)
            z_c = _dft_small_call(wri_c)
            c_mid, f_c, wd_c = _postmix_call(o_c, z_c, ctx, cg1, csh2, csc2, lw, rwt, rb, tm_ctx)
            ctx = _moe_call(f_c, wd_c, c_mid, cg2, lw, tm_ctx)
    return x
```
